```python
import jax, jax.numpy as jnp
from jax import lax
import numpy as np

D_MODEL = 2048
BATCH = 8
SEQ = 2048
DEPTH = 2

N_MIXERS = 2
EPS = 1e-6
ML_HEADS = 8
ML_DQK = 128
ML_DV = 256
ML_CHUNK = 64
GATE_CAP = 15.0
SW_HEADS = 32
SW_KV_HEADS = 4
SW_HEAD_DIM = 64
SW_WINDOW = 128
SW_BLOCK = 128
D_FF = 5632
CONV_WIDTH = 3

kernel_name = "hybrid_mlstm_swa_convffn_adaln"


def rms_norm(x, eps=EPS):
    xf = x.astype(jnp.float32)
    return (xf * lax.rsqrt(jnp.mean(xf * xf, axis=-1, keepdims=True) + eps)).astype(x.dtype)


def modulation(c, w_mod, b_mod):
    m = jax.nn.silu(c) @ w_mod + b_mod
    return jnp.split(m[:, None, :], 6, axis=-1)


def mlstm_mixer(h, w_in, gate_bias, h_norm, w_out):
    B, S, _ = h.shape
    H, dk, dv, L = ML_HEADS, ML_DQK, ML_DV, ML_CHUNK
    f32 = jnp.float32
    proj = h @ w_in
    q, k, v, o, g = jnp.split(proj, [H * dk, 2 * H * dk, 2 * H * dk + H * dv, 2 * H * dk + 2 * H * dv], axis=-1)
    g = g.astype(f32) + gate_bias.astype(f32)
    g = GATE_CAP * jnp.tanh(g / GATE_CAP)
    i_pre = g[..., :H]
    log_f = jax.nn.log_sigmoid(g[..., H:])
    q = q.reshape(B, S, H, dk).astype(f32)
    k = k.reshape(B, S, H, dk).astype(f32) * (dk ** -0.5)
    v = v.reshape(B, S, H, dv).astype(f32)
    NC = S // L

    def chunks(t):
        t = t.reshape(B, NC, L, H, *t.shape[3:])
        return jnp.moveaxis(t, (1, 3), (0, 2))

    causal = jnp.tril(jnp.ones((L, L), dtype=bool))

    def step(carry, xs):
        C, n, m = carry
        qc, kc, vc, ic, fc = xs
        b = jnp.cumsum(fc, axis=-1)
        dmat = jnp.where(causal, b[..., :, None] - b[..., None, :] + ic[..., None, :], -jnp.inf)
        m_inter = b + m[..., None]
        m_t = jnp.maximum(m_inter, jnp.max(dmat, axis=-1))
        w_inter = jnp.exp(m_inter - m_t)
        p = jnp.exp(dmat - m_t[..., None]) * jnp.einsum('bhtd,bhsd->bhts', qc, kc)
        num = w_inter[..., None] * jnp.einsum('bhtd,bhdv->bhtv', qc, C) + jnp.einsum('bhts,bhsv->bhtv', p, vc)
        den = w_inter * jnp.einsum('bhtd,bhd->bht', qc, n) + jnp.sum(p, axis=-1)
        h_c = num / jnp.maximum(jnp.abs(den), jnp.exp(-m_t))[..., None]
        total = b[..., -1]
        g_s = total[..., None] - b + ic
        m_new = jnp.maximum(total + m, jnp.max(g_s, axis=-1))
        w_s = jnp.exp(g_s - m_new[..., None])
        decay = jnp.exp(total + m - m_new)
        C_new = decay[..., None, None] * C + jnp.einsum('bhs,bhsd,bhsv->bhdv', w_s, kc, vc)
        n_new = decay[..., None] * n + jnp.einsum('bhs,bhsd->bhd', w_s, kc)
        return (C_new, n_new, m_new), h_c

    init = (jnp.zeros((B, H, dk, dv), f32), jnp.zeros((B, H, dk), f32), jnp.zeros((B, H), f32))
    _, hs = lax.scan(step, init, (chunks(q), chunks(k), chunks(v), chunks(i_pre), chunks(log_f)))
    hs = jnp.moveaxis(hs, (0, 2), (1, 3)).reshape(B, S, H, dv)
    hs = rms_norm(hs) * h_norm.astype(f32).reshape(H, dv)
    out = jax.nn.sigmoid(o.astype(f32)) * hs.reshape(B, S, H * dv)
    return out.astype(h.dtype) @ w_out


def swa_mixer(h, w_qkv, q_norm, k_norm, sinks, w_out):
    B, S, _ = h.shape
    Hq, Hk, hd, BLK = SW_HEADS, SW_KV_HEADS, SW_HEAD_DIM, SW_BLOCK
    G = Hq // Hk
    f32 = jnp.float32
    qkv = h @ w_qkv
    q, k, v = jnp.split(qkv, [Hq * hd, Hq * hd + Hk * hd], axis=-1)
    q = rms_norm(q.reshape(B, S, Hk, G, hd)) * q_norm
    k = rms_norm(k.reshape(B, S, Hk, hd)) * k_norm
    v = v.reshape(B, S, Hk, hd)
    NB = S // BLK
    q_blocks = q.reshape(B, NB, BLK, Hk, G, hd).swapaxes(0, 1)

    def band(t):
        tp = jnp.pad(t, ((0, 0), (BLK, 0), (0, 0), (0, 0)))
        tb = tp.reshape(B, NB + 1, BLK, Hk, hd)
        return jnp.concatenate([tb[:, :-1], tb[:, 1:]], axis=2).swapaxes(0, 1)

    r = jnp.arange(2 * BLK)
    i = jnp.arange(BLK)
    rel = BLK + i[:, None] - r[None, :]
    band_mask = (rel >= 0) & (rel < SW_WINDOW)
    scale = hd ** -0.5
    sink_logits = sinks.astype(f32).reshape(Hk, G)[None, :, :, None, None]

    def block_attn(args):
        j, qb, kb, vb = args
        valid = band_mask & (r[None, :] + j * BLK >= BLK)
        s = jnp.einsum('bikgd,brkd->bkgir', qb.astype(f32), kb.astype(f32)) * scale
        s = jnp.where(valid, s, -jnp.inf)
        sink = jnp.broadcast_to(sink_logits, s.shape[:-1] + (1,))
        p = jax.nn.softmax(jnp.concatenate([s, sink], axis=-1), axis=-1)[..., :-1]
        return jnp.einsum('bkgir,brkd->bikgd', p, vb.astype(f32)).astype(qb.dtype)

    out = lax.map(block_attn, (jnp.arange(NB), q_blocks, band(k), band(v)))
    out = out.swapaxes(0, 1).reshape(B, S, Hq * hd)
    return out @ w_out


def conv_ffn(h, w_up, conv_w, conv_b, w_down):
    S = h.shape[1]
    u = h @ w_up
    up = jnp.pad(u, ((0, 0), (CONV_WIDTH - 1, 0), (0, 0)))
    u = sum(up[:, j:j + S] * conv_w[j] for j in range(CONV_WIDTH)) + conv_b
    gate, val = jnp.split(u, 2, axis=-1)
    return (jax.nn.silu(gate) * val) @ w_down


def setup_inputs(seed: int = 0) -> dict:
    key = jax.random.key(seed)
    ks = jax.random.split(key, 24)
    D, F = D_MODEL, D_FF
    H = ML_HEADS
    ml_in = 2 * H * ML_DQK + 2 * H * ML_DV + 2 * H
    sw_in = (SW_HEADS + 2 * SW_KV_HEADS) * SW_HEAD_DIM

    def nrm(k, shape, scale):
        return jax.random.normal(k, shape, jnp.float32) * scale

    f_bias = jnp.linspace(3.0, 6.0, H, dtype=jnp.float32)
    gate_bias = jnp.concatenate([nrm(ks[4], (H,), 0.1), f_bias + nrm(ks[5], (H,), 0.1)])
    return {
        "x": nrm(ks[0], (BATCH, SEQ, D), 1.0),
        "c": nrm(ks[1], (BATCH, D), 1.0),
        "l0_w_mod": nrm(ks[2], (D, 6 * D), 0.5 * D ** -0.5),
        "l0_b_mod": nrm(ks[3], (6 * D,), 0.01),
        "l0_w_in": nrm(ks[6], (D, ml_in), D ** -0.5),
        "l0_gate_bias": gate_bias,
        "l0_h_norm": 1.0 + nrm(ks[7], (H * ML_DV,), 0.02),
        "l0_w_out": nrm(ks[8], (H * ML_DV, D), (H * ML_DV) ** -0.5),
        "l0_w_up": nrm(ks[9], (D, 2 * F), D ** -0.5),
        "l0_conv_w": nrm(ks[10], (CONV_WIDTH, 2 * F), CONV_WIDTH ** -0.5),
        "l0_conv_b": nrm(ks[11], (2 * F,), 0.01),
        "l0_w_down": nrm(ks[12], (F, D), F ** -0.5),
        "l1_w_mod": nrm(ks[13], (D, 6 * D), 0.5 * D ** -0.5),
        "l1_b_mod": nrm(ks[14], (6 * D,), 0.01),
        "l1_w_qkv": nrm(ks[15], (D, sw_in), D ** -0.5),
        "l1_q_norm": 1.0 + nrm(ks[16], (SW_HEAD_DIM,), 0.02),
        "l1_k_norm": 1.0 + nrm(ks[17], (SW_HEAD_DIM,), 0.02),
        "l1_sinks": nrm(ks[18], (SW_HEADS,), 1.0),
        "l1_w_out": nrm(ks[19], (SW_HEADS * SW_HEAD_DIM, D), (SW_HEADS * SW_HEAD_DIM) ** -0.5),
        "l1_w_up": nrm(ks[20], (D, 2 * F), D ** -0.5),
        "l1_conv_w": nrm(ks[21], (CONV_WIDTH, 2 * F), CONV_WIDTH ** -0.5),
        "l1_conv_b": nrm(ks[22], (2 * F,), 0.01),
        "l1_w_down": nrm(ks[23], (F, D), F ** -0.5),
    }


def reference(x, c, l0_w_mod, l0_b_mod, l0_w_in, l0_gate_bias, l0_h_norm, l0_w_out, l0_w_up, l0_conv_w, l0_conv_b, l0_w_down, l1_w_mod, l1_b_mod, l1_w_qkv, l1_q_norm, l1_k_norm, l1_sinks, l1_w_out, l1_w_up, l1_conv_w, l1_conv_b, l1_w_down):
    mod_params = [(l0_w_mod, l0_b_mod), (l1_w_mod, l1_b_mod)]
    ffn_params = [(l0_w_up, l0_conv_w, l0_conv_b, l0_w_down), (l1_w_up, l1_conv_w, l1_conv_b, l1_w_down)]
    for layer in range(DEPTH):
        sh1, sc1, g1, sh2, sc2, g2 = modulation(c, *mod_params[layer])
        hm = rms_norm(x) * (1.0 + sc1) + sh1
        if layer % N_MIXERS == 0:
            y = mlstm_mixer(hm, l0_w_in, l0_gate_bias, l0_h_norm, l0_w_out)
        else:
            y = swa_mixer(hm, l1_w_qkv, l1_q_norm, l1_k_norm, l1_sinks, l1_w_out)
        x = x + g1 * y
        hf = rms_norm(x) * (1.0 + sc2) + sh2
        x = x + g2 * conv_ffn(hf, *ffn_params[layer])
    return x
```

```python
import functools
import math

import jax
import jax.numpy as jnp
from jax import lax
from jax.experimental import pallas as pl
from jax.experimental.pallas import tpu as pltpu

EPS = 1e-6
ML_HEADS = 8
ML_DQK = 128
ML_DV = 256
GATE_CAP = 15.0
SW_HEADS = 32
SW_KV_HEADS = 4
SW_HEAD_DIM = 64
SW_BLOCK = 128
CONV_WIDTH = 3

LANES = 128
SUBLANES = 8
VMEM_LIMIT = 56 * 1024 * 1024

F32 = jnp.float32
BF16 = jnp.bfloat16


def _params(*sem):
    return pltpu.CompilerParams(dimension_semantics=sem, vmem_limit_bytes=VMEM_LIMIT)


def _dot(a, b):
    return jnp.dot(a, b, preferred_element_type=F32)


def _rms_mod(x, scale_row, shift_row):
    ms = jnp.mean(x * x, axis=-1, keepdims=True)
    return (x * lax.rsqrt(ms + EPS)) * (1.0 + scale_row) + shift_row


def _mod_kernel(c_ref, w_ref, b_ref, o_ref):
    c = c_ref[...]
    sc = (c * jax.nn.sigmoid(c)).astype(BF16)
    o_ref[...] = _dot(sc, w_ref[...].astype(BF16)) + b_ref[...]


def _modulation(c, w_mod, b_mod, tn=1024):
    B, D = c.shape
    N = w_mod.shape[1]
    return pl.pallas_call(
        _mod_kernel,
        grid=(N // tn,),
        in_specs=[pl.BlockSpec((B, D), lambda j: (0, 0)),
                  pl.BlockSpec((D, tn), lambda j: (0, j)),
                  pl.BlockSpec((1, tn), lambda j: (0, j))],
        out_specs=pl.BlockSpec((B, tn), lambda j: (0, j)),
        out_shape=jax.ShapeDtypeStruct((B, N), F32),
        compiler_params=_params("arbitrary"),
        name="modulation",
    )(c, w_mod, b_mod.reshape(1, N))


def _norm_kernel(x_ref, mod_ref, o_ref):
    o_ref[...] = _rms_mod(x_ref[...], mod_ref[0, 1:2, :], mod_ref[0, 0:1, :]).astype(BF16)


def _first_norm(x2d, mod, seq, tm=512):
    T, D = x2d.shape
    per = seq // tm
    return pl.pallas_call(
        _norm_kernel,
        grid=(T // tm,),
        in_specs=[pl.BlockSpec((tm, D), lambda i: (i, 0)),
                  pl.BlockSpec((1, 6, D), lambda i: (i // per, 0, 0))],
        out_specs=pl.BlockSpec((tm, D), lambda i: (i, 0)),
        out_shape=jax.ShapeDtypeStruct((T, D), BF16),
        compiler_params=_params("arbitrary"),
        name="first_norm",
    )(x2d, mod)


def _inproj_kernel(h_ref, w_ref, cs_ref, wg_ref, gb_ref, o_ref, gcol_ref, grow_ref):
    j = pl.program_id(0)
    h = h_ref[...]
    o_ref[...] = (_dot(h, w_ref[...]) * cs_ref[...]).astype(BF16)

    @pl.when(j == 0)
    def _():
        g = _dot(h, wg_ref[...]) + gb_ref[...]
        g = GATE_CAP * jnp.tanh(g * (1.0 / GATE_CAP))
        log_f = jnp.minimum(g, 0.0) - jnp.log(1.0 + jnp.exp(-jnp.abs(g)))
        lane = lax.broadcasted_iota(jnp.int32, g.shape, 1)
        g = jnp.where(lane < ML_HEADS, g, log_f)
        gcol_ref[...] = g
        grow_ref[...] = jnp.transpose(g)[: 2 * ML_HEADS, :]


def _inproj(h, w_main, col_scale, w_gate, gate_bias, tm=1024, tn=1024):
    T, D = h.shape
    N = w_main.shape[1]
    nm = T // tm
    return pl.pallas_call(
        _inproj_kernel,
        grid=(N // tn, nm),
        in_specs=[pl.BlockSpec((tm, D), lambda j, i: (i, 0)),
                  pl.BlockSpec((D, tn), lambda j, i: (0, j)),
                  pl.BlockSpec((1, tn), lambda j, i: (0, j)),
                  pl.BlockSpec((D, LANES), lambda j, i: (0, 0)),
                  pl.BlockSpec((1, LANES), lambda j, i: (0, 0))],
        out_specs=[pl.BlockSpec((tm, tn), lambda j, i: (i, j)),
                   pl.BlockSpec((tm, LANES), lambda j, i: (jnp.where(j == 0, i, nm - 1), 0)),
                   pl.BlockSpec((2 * ML_HEADS, tm), lambda j, i: (0, jnp.where(j == 0, i, nm - 1)))],
        out_shape=[jax.ShapeDtypeStruct((T, N), BF16),
                   jax.ShapeDtypeStruct((T, LANES), F32),
                   jax.ShapeDtypeStruct((2 * ML_HEADS, T), F32)],
        compiler_params=_params("arbitrary", "arbitrary"),
        name="mlstm_inproj",
    )(h, w_main, col_scale, w_gate, gate_bias)


def _mlstm_kernel(q_ref, k_ref, v_ref, o_ref, gcol_ref, grow_ref, hn_ref, y_ref, c_ref, n_ref, m_ref,
                  *, heads_per_step, chunk, seq):
    L = chunk
    hg = heads_per_step
    h0 = pl.program_id(1) * hg
    dk, dv = ML_DQK, ML_DV
    c_ref[...] = jnp.zeros_like(c_ref)
    n_ref[...] = jnp.zeros_like(n_ref)
    m_ref[...] = jnp.zeros_like(m_ref)
    row = lax.broadcasted_iota(jnp.int32, (L, L), 0)
    col = lax.broadcasted_iota(jnp.int32, (L, L), 1)
    causal = col <= row
    lane = lax.broadcasted_iota(jnp.int32, (1, LANES), 1)
    sub = lax.broadcasted_iota(jnp.int32, (2 * ML_HEADS, 1), 0)

    def chunk_step(ci, carry):
        r0 = pl.multiple_of(ci * L, L)
        gcol = gcol_ref[pl.ds(r0, L), :]
        grow = grow_ref[:, pl.ds(r0, L)]
        for hh in range(hg):
            head = h0 + hh
            i_col = jnp.sum(jnp.where(lane == head, gcol, 0.0), axis=1, keepdims=True)
            f_col = jnp.sum(jnp.where(lane == head + ML_HEADS, gcol, 0.0), axis=1, keepdims=True)
            i_row = jnp.sum(jnp.where(sub == head, grow, 0.0), axis=0, keepdims=True)
            f_row = jnp.sum(jnp.where(sub == head + ML_HEADS, grow, 0.0), axis=0, keepdims=True)
            b_col = jnp.sum(jnp.where(causal, f_row, 0.0), axis=1, keepdims=True)
            b_row = jnp.sum(jnp.where(causal, 0.0, f_col), axis=0, keepdims=True) + f_row
            total = jnp.sum(f_row, axis=1, keepdims=True)
            a_row = i_row - b_row
            a_col = i_col - b_col
            m_prev = m_ref[hh]
            dmat = jnp.where(causal, b_col + a_row, -jnp.inf)
            m_inter = b_col + m_prev
            m_t = jnp.maximum(m_inter, jnp.max(dmat, axis=1, keepdims=True))
            w_inter = jnp.exp(m_inter - m_t)
            q = q_ref[pl.ds(r0, L), hh * dk:(hh + 1) * dk]
            k = k_ref[pl.ds(r0, L), hh * dk:(hh + 1) * dk]
            v = v_ref[pl.ds(r0, L), hh * dv:(hh + 1) * dv]
            s = lax.dot_general(q, k, (((1,), (1,)), ((), ())), preferred_element_type=F32)
            p = jnp.exp(dmat - m_t) * s
            c_state = c_ref[hh]
            n_state = n_ref[hh]
            num = w_inter * _dot(q, c_state.astype(BF16)) + _dot(p.astype(BF16), v)
            qn = jnp.sum(q.astype(F32) * n_state, axis=1, keepdims=True)
            den = w_inter * qn + jnp.sum(p, axis=1, keepdims=True)
            hc = num / jnp.maximum(jnp.abs(den), jnp.exp(-m_t))
            hs = hc * lax.rsqrt(jnp.mean(hc * hc, axis=1, keepdims=True) + EPS) * hn_ref[:, hh * dv:(hh + 1) * dv]
            og = o_ref[pl.ds(r0, L), hh * dv:(hh + 1) * dv].astype(F32)
            y_ref[pl.ds(r0, L), hh * dv:(hh + 1) * dv] = (jax.nn.sigmoid(og) * hs).astype(BF16)
            g_col = total + a_col
            m_new = jnp.maximum(total + m_prev, jnp.max(g_col, axis=0, keepdims=True))
            w_col = jnp.exp(g_col - m_new)
            decay = jnp.exp(total + m_prev - m_new)
            k_w = k.astype(F32) * w_col
            c_ref[hh] = decay * c_state + lax.dot_general(
                k_w.astype(BF16), v, (((0,), (0,)), ((), ())), preferred_element_type=F32)
            n_ref[hh] = decay * n_state + jnp.sum(k_w, axis=0, keepdims=True)
            m_ref[hh] = m_new
        return carry

    lax.fori_loop(0, seq // L, chunk_step, 0)


def _mlstm(qkvo, gcol, grow, h_norm, batch, seq, heads_per_step=2, chunk=128):
    T = qkvo.shape[0]
    H, dk, dv = ML_HEADS, ML_DQK, ML_DV
    hg = heads_per_step
    qb, vb = hg * dk, hg * dv
    k_off = (H * dk) // qb
    v_off = (2 * H * dk) // vb
    o_off = (2 * H * dk + H * dv) // vb
    kern = functools.partial(_mlstm_kernel, heads_per_step=hg, chunk=chunk, seq=seq)
    return pl.pallas_call(
        kern,
        grid=(batch, H // hg),
        in_specs=[pl.BlockSpec((seq, qb), lambda b, g: (b, g)),
                  pl.BlockSpec((seq, qb), lambda b, g: (b, k_off + g)),
                  pl.BlockSpec((seq, vb), lambda b, g: (b, v_off + g)),
                  pl.BlockSpec((seq, vb), lambda b, g: (b, o_off + g)),
                  pl.BlockSpec((seq, LANES), lambda b, g: (b, 0)),
                  pl.BlockSpec((2 * H, seq), lambda b, g: (0, b)),
                  pl.BlockSpec((1, vb), lambda b, g: (0, g))],
        out_specs=pl.BlockSpec((seq, vb), lambda b, g: (b, g)),
        out_shape=jax.ShapeDtypeStruct((T, H * dv), BF16),
        scratch_shapes=[pltpu.VMEM((hg, dk, dv), F32),
                        pltpu.VMEM((hg, 1, dk), F32),
                        pltpu.VMEM((hg, 1, 1), F32)],
        compiler_params=_params("arbitrary", "arbitrary"),
        name="mlstm_chunks",
    )(qkvo, qkvo, qkvo, qkvo, gcol, grow, h_norm)


def _outproj_kernel(y_ref, w_ref, x_ref, mod_ref, xo_ref, ho_ref):
    acc = _dot(y_ref[...], w_ref[...])
    x_new = x_ref[...] + mod_ref[0, 2:3, :] * acc
    xo_ref[...] = x_new
    ho_ref[...] = _rms_mod(x_new, mod_ref[0, 4:5, :], mod_ref[0, 3:4, :]).astype(BF16)


def _outproj(y, w, x2d, mod, seq, tm=512):
    T, K = y.shape
    D = w.shape[1]
    per = seq // tm
    return pl.pallas_call(
        _outproj_kernel,
        grid=(T // tm,),
        in_specs=[pl.BlockSpec((tm, K), lambda i: (i, 0)),
                  pl.BlockSpec((K, D), lambda i: (0, 0)),
                  pl.BlockSpec((tm, D), lambda i: (i, 0)),
                  pl.BlockSpec((1, 6, D), lambda i: (i // per, 0, 0))],
        out_specs=[pl.BlockSpec((tm, D), lambda i: (i, 0)),
                   pl.BlockSpec((tm, D), lambda i: (i, 0))],
        out_shape=[jax.ShapeDtypeStruct((T, D), F32),
                   jax.ShapeDtypeStruct((T, D), BF16)],
        compiler_params=_params("arbitrary"),
        name="outproj_residual_norm",
    )(y, w, x2d, mod)


def _ffn_up_kernel(h_ref, wg_ref, wv_ref, cwg_ref, cwv_ref, cbg_ref, cbv_ref, o_ref, carry_g, carry_v,
                   *, tm, tiles_per_seq):
    i = pl.program_id(1)
    h = h_ref[...]
    seq_start = (i % tiles_per_seq) == 0

    @pl.when(seq_start)
    def _():
        carry_g[...] = jnp.zeros_like(carry_g)
        carry_v[...] = jnp.zeros_like(carry_v)

    def conv(u, carry_ref, cw_ref, cb_ref):
        prev = carry_ref[...]
        carry_ref[...] = u[tm - SUBLANES:, :]
        ext = jnp.concatenate([prev, u], axis=0)
        out = cb_ref[...] + cw_ref[2:3, :] * u
        out = out + cw_ref[1:2, :] * ext[SUBLANES - 1:SUBLANES - 1 + tm, :]
        out = out + cw_ref[0:1, :] * ext[SUBLANES - 2:SUBLANES - 2 + tm, :]
        return out

    g = conv(_dot(h, wg_ref[...]), carry_g, cwg_ref, cbg_ref)
    v = conv(_dot(h, wv_ref[...]), carry_v, cwv_ref, cbv_ref)
    o_ref[...] = (g * jax.nn.sigmoid(g) * v).astype(BF16)


def _ffn_up(h, w_up, conv_w, conv_b, seq, tm=1024, tn=512):
    T, D = h.shape
    F = w_up.shape[1] // 2
    nf = F // tn
    kern = functools.partial(_ffn_up_kernel, tm=tm, tiles_per_seq=seq // tm)
    cb = conv_b.reshape(1, 2 * F)
    return pl.pallas_call(
        kern,
        grid=(nf, T // tm),
        in_specs=[pl.BlockSpec((tm, D), lambda j, i: (i, 0)),
                  pl.BlockSpec((D, tn), lambda j, i: (0, j)),
                  pl.BlockSpec((D, tn), lambda j, i: (0, nf + j)),
                  pl.BlockSpec((CONV_WIDTH, tn), lambda j, i: (0, j)),
                  pl.BlockSpec((CONV_WIDTH, tn), lambda j, i: (0, nf + j)),
                  pl.BlockSpec((1, tn), lambda j, i: (0, j)),
                  pl.BlockSpec((1, tn), lambda j, i: (0, nf + j))],
        out_specs=pl.BlockSpec((tm, tn), lambda j, i: (i, j)),
        out_shape=jax.ShapeDtypeStruct((T, F), BF16),
        scratch_shapes=[pltpu.VMEM((SUBLANES, tn), F32), pltpu.VMEM((SUBLANES, tn), F32)],
        compiler_params=_params("arbitrary", "arbitrary"),
        name="ffn_up_conv_act",
    )(h, w_up, w_up, conv_w, conv_w, cb, cb)


def _ffn_down_kernel(a_ref, w_ref, x_ref, mod_ref, nmod_ref, xo_ref, ho_ref, acc_ref, *, nk):
    k = pl.program_id(1)
    part = _dot(a_ref[...], w_ref[...])

    @pl.when(k == 0)
    def _():
        acc_ref[...] = part

    @pl.when(k > 0)
    def _():
        acc_ref[...] += part

    @pl.when(k == nk - 1)
    def _():
        x_new = x_ref[...] + mod_ref[0, 5:6, :] * acc_ref[...]
        xo_ref[...] = x_new
        ho_ref[...] = _rms_mod(x_new, nmod_ref[0, 1:2, :], nmod_ref[0, 0:1, :]).astype(BF16)


def _ffn_down(a, w, x2d, mod, next_mod, seq, tm=512, tk=1408):
    T, K = a.shape
    D = w.shape[1]
    per = seq // tm
    nk = K // tk
    kern = functools.partial(_ffn_down_kernel, nk=nk)
    return pl.pallas_call(
        kern,
        grid=(T // tm, nk),
        in_specs=[pl.BlockSpec((tm, tk), lambda i, k: (i, k)),
                  pl.BlockSpec((tk, D), lambda i, k: (k, 0)),
                  pl.BlockSpec((tm, D), lambda i, k: (i, 0)),
                  pl.BlockSpec((1, 6, D), lambda i, k: (i // per, 0, 0)),
                  pl.BlockSpec((1, 6, D), lambda i, k: (i // per, 0, 0))],
        out_specs=[pl.BlockSpec((tm, D), lambda i, k: (i, 0)),
                   pl.BlockSpec((tm, D), lambda i, k: (i, 0))],
        out_shape=[jax.ShapeDtypeStruct((T, D), F32),
                   jax.ShapeDtypeStruct((T, D), BF16)],
        scratch_shapes=[pltpu.VMEM((tm, D), F32)],
        compiler_params=_params("arbitrary", "arbitrary"),
        name="ffn_down_residual_norm",
    )(a, w, x2d, mod, next_mod)


def _qkv_kernel(h_ref, w_ref, gn_ref, blk_ref, o_ref, *, n_norm_tiles):
    j = pl.program_id(0)
    acc = _dot(h_ref[...], w_ref[...])

    @pl.when(j < n_norm_tiles)
    def _():
        sq = acc * acc
        hi = sq.astype(BF16)
        lo = (sq - hi.astype(F32)).astype(BF16)
        ss = _dot(hi, blk_ref[...]) + _dot(lo, blk_ref[...])
        o_ref[...] = (acc * lax.rsqrt(ss * (1.0 / SW_HEAD_DIM) + EPS) * gn_ref[...]).astype(BF16)

    @pl.when(j >= n_norm_tiles)
    def _():
        o_ref[...] = acc.astype(BF16)


def _qkv_proj(h, w, gain_row, tm=1024, tn=256):
    T, D = h.shape
    N = w.shape[1]
    n_norm_tiles = (SW_HEADS + SW_KV_HEADS) * SW_HEAD_DIM // tn
    r = lax.broadcasted_iota(jnp.int32, (tn, tn), 0) // SW_HEAD_DIM
    c = lax.broadcasted_iota(jnp.int32, (tn, tn), 1) // SW_HEAD_DIM
    blk = (r == c).astype(BF16)
    kern = functools.partial(_qkv_kernel, n_norm_tiles=n_norm_tiles)
    return pl.pallas_call(
        kern,
        grid=(N // tn, T // tm),
        in_specs=[pl.BlockSpec((tm, D), lambda j, i: (i, 0)),
                  pl.BlockSpec((D, tn), lambda j, i: (0, j)),
                  pl.BlockSpec((1, tn), lambda j, i: (0, j)),
                  pl.BlockSpec((tn, tn), lambda j, i: (0, 0))],
        out_specs=pl.BlockSpec((tm, tn), lambda j, i: (i, j)),
        out_shape=jax.ShapeDtypeStruct((T, N), BF16),
        compiler_params=_params("arbitrary", "arbitrary"),
        name="swa_qkv_proj",
    )(h, w, gain_row, blk)


def _swa_kernel(q_ref, kp_ref, kc_ref, vp_ref, vc_ref, sink_ref, o_ref):
    blk = pl.program_id(1)
    BLK, hd = SW_BLOCK, SW_HEAD_DIM
    G = SW_HEADS // SW_KV_HEADS
    qi = lax.broadcasted_iota(jnp.int32, (BLK, 2 * BLK), 0)
    kr = lax.broadcasted_iota(jnp.int32, (BLK, 2 * BLK), 1)
    rel = BLK + qi - kr
    valid = (rel >= 0) & (rel < BLK) & ((kr >= BLK) | (blk > 0))
    for kh in range(SW_KV_HEADS):
        kband = jnp.concatenate([kp_ref[:, kh * hd:(kh + 1) * hd], kc_ref[:, kh * hd:(kh + 1) * hd]], axis=0)
        vband = jnp.concatenate([vp_ref[:, kh * hd:(kh + 1) * hd], vc_ref[:, kh * hd:(kh + 1) * hd]], axis=0)
        for g in range(G):
            h = kh * G + g
            q = q_ref[:, h * hd:(h + 1) * hd]
            s = lax.dot_general(q, kband, (((1,), (1,)), ((), ())), preferred_element_type=F32)
            s = jnp.where(valid, s, -jnp.inf)
            sink = sink_ref[:, h:h + 1]
            mx = jnp.maximum(jnp.max(s, axis=1, keepdims=True), sink)
            e = jnp.exp(s - mx)
            den = jnp.sum(e, axis=1, keepdims=True) + jnp.exp(sink - mx)
            o = _dot(e.astype(BF16), vband) / den
            o_ref[:, h * hd:(h + 1) * hd] = o.astype(BF16)


def _swa(qkv, sinks_row, batch, seq):
    T = qkv.shape[0]
    BLK = SW_BLOCK
    nb = seq // BLK
    qw = SW_HEADS * SW_HEAD_DIM
    kw = SW_KV_HEADS * SW_HEAD_DIM
    k_off = qw // kw
    v_off = k_off + 1

    def cur(b, j):
        return b * nb + j

    def prev(b, j):
        return b * nb + jnp.maximum(j - 1, 0)

    return pl.pallas_call(
        _swa_kernel,
        grid=(batch, nb),
        in_specs=[pl.BlockSpec((BLK, qw), lambda b, j: (cur(b, j), 0)),
                  pl.BlockSpec((BLK, kw), lambda b, j: (prev(b, j), k_off)),
                  pl.BlockSpec((BLK, kw), lambda b, j: (cur(b, j), k_off)),
                  pl.BlockSpec((BLK, kw), lambda b, j: (prev(b, j), v_off)),
                  pl.BlockSpec((BLK, kw), lambda b, j: (cur(b, j), v_off)),
                  pl.BlockSpec((1, SW_HEADS), lambda b, j: (0, 0))],
        out_specs=pl.BlockSpec((BLK, qw), lambda b, j: (cur(b, j), 0)),
        out_shape=jax.ShapeDtypeStruct((T, qw), BF16),
        compiler_params=_params("arbitrary", "arbitrary"),
        name="swa_attention",
    )(qkv, qkv, qkv, qkv, qkv, sinks_row)


def kernel(x, c, l0_w_mod, l0_b_mod, l0_w_in, l0_gate_bias, l0_h_norm, l0_w_out, l0_w_up, l0_conv_w, l0_conv_b,
           l0_w_down, l1_w_mod, l1_b_mod, l1_w_qkv, l1_q_norm, l1_k_norm, l1_sinks, l1_w_out, l1_w_up, l1_conv_w,
           l1_conv_b, l1_w_down):
    B, S, D = x.shape
    T = B * S
    H, dk, dv = ML_HEADS, ML_DQK, ML_DV
    x2d = x.reshape(T, D)

    mod0 = _modulation(c, l0_w_mod, l0_b_mod).reshape(B, 6, D)
    mod1 = _modulation(c, l1_w_mod, l1_b_mod).reshape(B, 6, D)

    n_main = 2 * H * dk + 2 * H * dv
    w_main = l0_w_in[:, :n_main].astype(BF16)
    w_gate = jnp.pad(l0_w_in[:, n_main:], ((0, 0), (0, LANES - 2 * H))).astype(BF16)
    gate_bias = jnp.pad(l0_gate_bias, (0, LANES - 2 * H)).reshape(1, LANES)
    col_scale = jnp.concatenate([jnp.ones((H * dk,), F32), jnp.full((H * dk,), dk ** -0.5, F32),
                                 jnp.ones((2 * H * dv,), F32)]).reshape(1, n_main)
    hm = _first_norm(x2d, mod0, S)
    qkvo, gcol, grow = _inproj(hm, w_main, col_scale, w_gate, gate_bias)
    y = _mlstm(qkvo, gcol, grow, l0_h_norm.reshape(1, H * dv), B, S)
    x1, hf = _outproj(y, l0_w_out.astype(BF16), x2d, mod0, S)
    act = _ffn_up(hf, l0_w_up.astype(BF16), l0_conv_w, l0_conv_b, S)
    x2, hm1 = _ffn_down(act, l0_w_down.astype(BF16), x1, mod0, mod1, S)

    scale = SW_HEAD_DIM ** -0.5
    gain = jnp.concatenate([jnp.tile(l1_q_norm * scale, SW_HEADS), jnp.tile(l1_k_norm, SW_KV_HEADS),
                            jnp.ones((SW_KV_HEADS * SW_HEAD_DIM,), F32)]).reshape(1, -1)
    qkv = _qkv_proj(hm1, l1_w_qkv.astype(BF16), gain)
    att = _swa(qkv, l1_sinks.reshape(1, SW_HEADS), B, S)
    x3, hf1 = _outproj(att, l1_w_out.astype(BF16), x2, mod1, S)
    act1 = _ffn_up(hf1, l1_w_up.astype(BF16), l1_conv_w, l1_conv_b, S)
    x4, _ = _ffn_down(act1, l1_w_down.astype(BF16), x3, mod1, mod1, S)
    return x4.reshape(B, S, D)
```

```python
import functools
import math

import jax
import jax.numpy as jnp
from jax import lax
from jax.experimental import pallas as pl
from jax.experimental.pallas import tpu as pltpu

EPS = 1e-6
ML_HEADS = 8
ML_DQK = 128
ML_DV = 256
GATE_CAP = 15.0
SW_HEADS = 32
SW_KV_HEADS = 4
SW_HEAD_DIM = 64
SW_BLOCK = 128
CONV_WIDTH = 3
SWA_PAIRS_PER_DOT = 1

LANES = 128
SUBLANES = 8
VMEM_LIMIT = 56 * 1024 * 1024

F32 = jnp.float32
BF16 = jnp.bfloat16


def _params(*sem):
    return pltpu.CompilerParams(dimension_semantics=sem, vmem_limit_bytes=VMEM_LIMIT)


def _dot(a, b):
    return jnp.dot(a, b, preferred_element_type=F32)


def _rms_mod(x, scale_row, shift_row):
    ms = jnp.mean(x * x, axis=-1, keepdims=True)
    return (x * lax.rsqrt(ms + EPS)) * (1.0 + scale_row) + shift_row


def _mod_kernel(c_ref, w_ref, b_ref, o_ref):
    c = c_ref[...]
    sc = (c * jax.nn.sigmoid(c)).astype(BF16)
    o_ref[...] = _dot(sc, w_ref[...].astype(BF16)) + b_ref[...]


def _modulation(c, w_mod, b_mod, tn=1024):
    B, D = c.shape
    N = w_mod.shape[1]
    return pl.pallas_call(
        _mod_kernel,
        grid=(N // tn,),
        in_specs=[pl.BlockSpec((B, D), lambda j: (0, 0)),
                  pl.BlockSpec((D, tn), lambda j: (0, j)),
                  pl.BlockSpec((1, tn), lambda j: (0, j))],
        out_specs=pl.BlockSpec((B, tn), lambda j: (0, j)),
        out_shape=jax.ShapeDtypeStruct((B, N), F32),
        compiler_params=_params("arbitrary"),
        name="modulation",
    )(c, w_mod, b_mod.reshape(1, N))


def _norm_kernel(x_ref, mod_ref, o_ref):
    o_ref[...] = _rms_mod(x_ref[...], mod_ref[0, 1:2, :], mod_ref[0, 0:1, :]).astype(BF16)


def _first_norm(x2d, mod, seq, tm=512):
    T, D = x2d.shape
    per = seq // tm
    return pl.pallas_call(
        _norm_kernel,
        grid=(T // tm,),
        in_specs=[pl.BlockSpec((tm, D), lambda i: (i, 0)),
                  pl.BlockSpec((1, 6, D), lambda i: (i // per, 0, 0))],
        out_specs=pl.BlockSpec((tm, D), lambda i: (i, 0)),
        out_shape=jax.ShapeDtypeStruct((T, D), BF16),
        compiler_params=_params("arbitrary"),
        name="first_norm",
    )(x2d, mod)


def _inproj_kernel(h_ref, w_ref, cs_ref, wg_ref, gb_ref, o_ref, gcol_ref, grow_ref):
    j = pl.program_id(0)
    h = h_ref[...]
    o_ref[...] = (_dot(h, w_ref[...]) * cs_ref[...]).astype(BF16)

    @pl.when(j == 0)
    def _():
        g = _dot(h, wg_ref[...]) + gb_ref[...]
        g = GATE_CAP * jnp.tanh(g * (1.0 / GATE_CAP))
        log_f = jnp.minimum(g, 0.0) - jnp.log(1.0 + jnp.exp(-jnp.abs(g)))
        lane = lax.broadcasted_iota(jnp.int32, g.shape, 1)
        g = jnp.where(lane < ML_HEADS, g, log_f)
        gcol_ref[...] = g
        grow_ref[...] = jnp.transpose(g)[: 2 * ML_HEADS, :]


def _inproj(h, w_main, col_scale, w_gate, gate_bias, tm=1024, tn=1024):
    T, D = h.shape
    N = w_main.shape[1]
    nm = T // tm
    return pl.pallas_call(
        _inproj_kernel,
        grid=(N // tn, nm),
        in_specs=[pl.BlockSpec((tm, D), lambda j, i: (i, 0)),
                  pl.BlockSpec((D, tn), lambda j, i: (0, j)),
                  pl.BlockSpec((1, tn), lambda j, i: (0, j)),
                  pl.BlockSpec((D, LANES), lambda j, i: (0, 0)),
                  pl.BlockSpec((1, LANES), lambda j, i: (0, 0))],
        out_specs=[pl.BlockSpec((tm, tn), lambda j, i: (i, j)),
                   pl.BlockSpec((tm, LANES), lambda j, i: (jnp.where(j == 0, i, nm - 1), 0)),
                   pl.BlockSpec((2 * ML_HEADS, tm), lambda j, i: (0, jnp.where(j == 0, i, nm - 1)))],
        out_shape=[jax.ShapeDtypeStruct((T, N), BF16),
                   jax.ShapeDtypeStruct((T, LANES), F32),
                   jax.ShapeDtypeStruct((2 * ML_HEADS, T), F32)],
        compiler_params=_params("arbitrary", "arbitrary"),
        name="mlstm_inproj",
    )(h, w_main, col_scale, w_gate, gate_bias)


def _mlstm_kernel(q_ref, k_ref, v_ref, o_ref, gcol_ref, grow_ref, hn_ref, y_ref, c_ref, n_ref, m_ref,
                  *, heads_per_step, chunk, seq):
    L = chunk
    hg = heads_per_step
    h0 = pl.program_id(1) * hg
    dk, dv = ML_DQK, ML_DV
    c_ref[...] = jnp.zeros_like(c_ref)
    n_ref[...] = jnp.zeros_like(n_ref)
    m_ref[...] = jnp.zeros_like(m_ref)
    row = lax.broadcasted_iota(jnp.int32, (L, L), 0)
    col = lax.broadcasted_iota(jnp.int32, (L, L), 1)
    causal = col <= row
    lane = lax.broadcasted_iota(jnp.int32, (1, LANES), 1)
    sub = lax.broadcasted_iota(jnp.int32, (2 * ML_HEADS, 1), 0)

    def chunk_step(ci, carry):
        r0 = pl.multiple_of(ci * L, L)
        gcol = gcol_ref[pl.ds(r0, L), :]
        grow = grow_ref[:, pl.ds(r0, L)]
        for hh in range(hg):
            head = h0 + hh
            i_col = jnp.sum(jnp.where(lane == head, gcol, 0.0), axis=1, keepdims=True)
            f_col = jnp.sum(jnp.where(lane == head + ML_HEADS, gcol, 0.0), axis=1, keepdims=True)
            i_row = jnp.sum(jnp.where(sub == head, grow, 0.0), axis=0, keepdims=True)
            f_row = jnp.sum(jnp.where(sub == head + ML_HEADS, grow, 0.0), axis=0, keepdims=True)
            b_col = jnp.sum(jnp.where(causal, f_row, 0.0), axis=1, keepdims=True)
            b_row = jnp.sum(jnp.where(causal, 0.0, f_col), axis=0, keepdims=True) + f_row
            total = jnp.sum(f_row, axis=1, keepdims=True)
            a_row = i_row - b_row
            a_col = i_col - b_col
            m_prev = m_ref[hh]
            dmat = jnp.where(causal, b_col + a_row, -jnp.inf)
            m_inter = b_col + m_prev
            m_t = jnp.maximum(m_inter, jnp.max(dmat, axis=1, keepdims=True))
            w_inter = jnp.exp(m_inter - m_t)
            g_col = total + a_col
            m_new = jnp.maximum(total + m_prev, jnp.max(g_col, axis=0, keepdims=True))
            w_col = jnp.exp(g_col - m_new)
            decay = jnp.exp(total + m_prev - m_new)
            q = q_ref[pl.ds(r0, L), hh * dk:(hh + 1) * dk]
            k = k_ref[pl.ds(r0, L), hh * dk:(hh + 1) * dk]
            v = v_ref[pl.ds(r0, L), hh * dv:(hh + 1) * dv]
            c_state = c_ref[hh]
            n_state = n_ref[hh]
            s = lax.dot_general(q, k, (((1,), (1,)), ((), ())), preferred_element_type=F32)
            p = jnp.exp(dmat - m_t) * s
            num = w_inter * _dot(q, c_state.astype(BF16)) + _dot(p.astype(BF16), v)
            qn = jnp.sum(q.astype(F32) * n_state, axis=1, keepdims=True)
            den = w_inter * qn + jnp.sum(p, axis=1, keepdims=True)
            hc = num / jnp.maximum(jnp.abs(den), jnp.exp(-m_t))
            hs = hc * lax.rsqrt(jnp.mean(hc * hc, axis=1, keepdims=True) + EPS) * hn_ref[:, hh * dv:(hh + 1) * dv]
            og = o_ref[pl.ds(r0, L), hh * dv:(hh + 1) * dv].astype(F32)
            y_ref[pl.ds(r0, L), hh * dv:(hh + 1) * dv] = (jax.nn.sigmoid(og) * hs).astype(BF16)
            k_w = k.astype(F32) * w_col
            c_ref[hh] = decay * c_state + lax.dot_general(
                k_w.astype(BF16), v, (((0,), (0,)), ((), ())), preferred_element_type=F32)
            n_ref[hh] = decay * n_state + jnp.sum(k_w, axis=0, keepdims=True)
            m_ref[hh] = m_new
        return carry

    lax.fori_loop(0, seq // L, chunk_step, 0)


def _mlstm(qkvo, gcol, grow, h_norm, batch, seq, heads_per_step=4, chunk=128):
    T = qkvo.shape[0]
    H, dk, dv = ML_HEADS, ML_DQK, ML_DV
    hg = heads_per_step
    qb, vb = hg * dk, hg * dv
    k_off = (H * dk) // qb
    v_off = (2 * H * dk) // vb
    o_off = (2 * H * dk + H * dv) // vb
    kern = functools.partial(_mlstm_kernel, heads_per_step=hg, chunk=chunk, seq=seq)
    return pl.pallas_call(
        kern,
        grid=(batch, H // hg),
        in_specs=[pl.BlockSpec((seq, qb), lambda b, g: (b, g)),
                  pl.BlockSpec((seq, qb), lambda b, g: (b, k_off + g)),
                  pl.BlockSpec((seq, vb), lambda b, g: (b, v_off + g)),
                  pl.BlockSpec((seq, vb), lambda b, g: (b, o_off + g)),
                  pl.BlockSpec((seq, LANES), lambda b, g: (b, 0)),
                  pl.BlockSpec((2 * H, seq), lambda b, g: (0, b)),
                  pl.BlockSpec((1, vb), lambda b, g: (0, g))],
        out_specs=pl.BlockSpec((seq, vb), lambda b, g: (b, g)),
        out_shape=jax.ShapeDtypeStruct((T, H * dv), BF16),
        scratch_shapes=[pltpu.VMEM((hg, dk, dv), F32),
                        pltpu.VMEM((hg, 1, dk), F32),
                        pltpu.VMEM((hg, 1, 1), F32)],
        compiler_params=_params("arbitrary", "arbitrary"),
        name="mlstm_chunks",
    )(qkvo, qkvo, qkvo, qkvo, gcol, grow, h_norm)


def _outproj_kernel(y_ref, w_ref, x_ref, mod_ref, xo_ref, ho_ref):
    acc = _dot(y_ref[...], w_ref[...])
    x_new = x_ref[...] + mod_ref[0, 2:3, :] * acc
    xo_ref[...] = x_new
    ho_ref[...] = _rms_mod(x_new, mod_ref[0, 4:5, :], mod_ref[0, 3:4, :]).astype(BF16)


def _outproj(y, w, x2d, mod, seq, tm=512):
    T, K = y.shape
    D = w.shape[1]
    per = seq // tm
    return pl.pallas_call(
        _outproj_kernel,
        grid=(T // tm,),
        in_specs=[pl.BlockSpec((tm, K), lambda i: (i, 0)),
                  pl.BlockSpec((K, D), lambda i: (0, 0)),
                  pl.BlockSpec((tm, D), lambda i: (i, 0)),
                  pl.BlockSpec((1, 6, D), lambda i: (i // per, 0, 0))],
        out_specs=[pl.BlockSpec((tm, D), lambda i: (i, 0)),
                   pl.BlockSpec((tm, D), lambda i: (i, 0))],
        out_shape=[jax.ShapeDtypeStruct((T, D), F32),
                   jax.ShapeDtypeStruct((T, D), BF16)],
        compiler_params=_params("arbitrary"),
        name="outproj_residual_norm",
    )(y, w, x2d, mod)


def _ffn_up_kernel(h_ref, wg_ref, wv_ref, cwg_ref, cwv_ref, cbg_ref, cbv_ref, o_ref, carry_g, carry_v,
                   *, tm, tiles_per_seq):
    i = pl.program_id(1)
    h = h_ref[...]
    seq_start = (i % tiles_per_seq) == 0

    @pl.when(seq_start)
    def _():
        carry_g[...] = jnp.zeros_like(carry_g)
        carry_v[...] = jnp.zeros_like(carry_v)

    def conv(u, carry_ref, cw_ref, cb_ref):
        prev = carry_ref[...]
        carry_ref[...] = u[tm - SUBLANES:, :]
        ext = jnp.concatenate([prev, u], axis=0)
        out = cb_ref[...] + cw_ref[2:3, :] * u
        out = out + cw_ref[1:2, :] * ext[SUBLANES - 1:SUBLANES - 1 + tm, :]
        out = out + cw_ref[0:1, :] * ext[SUBLANES - 2:SUBLANES - 2 + tm, :]
        return out

    g = conv(_dot(h, wg_ref[...]), carry_g, cwg_ref, cbg_ref)
    v = conv(_dot(h, wv_ref[...]), carry_v, cwv_ref, cbv_ref)
    o_ref[...] = (g * jax.nn.sigmoid(g) * v).astype(BF16)


def _ffn_up(h, w_up, conv_w, conv_b, seq, tm=1024, tn=512):
    T, D = h.shape
    F = w_up.shape[1] // 2
    nf = F // tn
    kern = functools.partial(_ffn_up_kernel, tm=tm, tiles_per_seq=seq // tm)
    cb = conv_b.reshape(1, 2 * F)
    return pl.pallas_call(
        kern,
        grid=(nf, T // tm),
        in_specs=[pl.BlockSpec((tm, D), lambda j, i: (i, 0)),
                  pl.BlockSpec((D, tn), lambda j, i: (0, j)),
                  pl.BlockSpec((D, tn), lambda j, i: (0, nf + j)),
                  pl.BlockSpec((CONV_WIDTH, tn), lambda j, i: (0, j)),
                  pl.BlockSpec((CONV_WIDTH, tn), lambda j, i: (0, nf + j)),
                  pl.BlockSpec((1, tn), lambda j, i: (0, j)),
                  pl.BlockSpec((1, tn), lambda j, i: (0, nf + j))],
        out_specs=pl.BlockSpec((tm, tn), lambda j, i: (i, j)),
        out_shape=jax.ShapeDtypeStruct((T, F), BF16),
        scratch_shapes=[pltpu.VMEM((SUBLANES, tn), F32), pltpu.VMEM((SUBLANES, tn), F32)],
        compiler_params=_params("arbitrary", "arbitrary"),
        name="ffn_up_conv_act",
    )(h, w_up, w_up, conv_w, conv_w, cb, cb)


def _ffn_down_kernel(a_ref, w_ref, x_ref, mod_ref, *rest, emit_norm):
    x_new = x_ref[...] + mod_ref[0, 5:6, :] * _dot(a_ref[...], w_ref[...])
    if emit_norm:
        nmod_ref, xo_ref, ho_ref = rest
        ho_ref[...] = _rms_mod(x_new, nmod_ref[0, 1:2, :], nmod_ref[0, 0:1, :]).astype(BF16)
    else:
        (xo_ref,) = rest
    xo_ref[...] = x_new


def _ffn_down(a, w, x2d, mod, next_mod, seq, tm=256):
    T, K = a.shape
    D = w.shape[1]
    per = seq // tm
    emit_norm = next_mod is not None
    mod_spec = pl.BlockSpec((1, 6, D), lambda i: (i // per, 0, 0))
    row_spec = pl.BlockSpec((tm, D), lambda i: (i, 0))
    in_specs = [pl.BlockSpec((tm, K), lambda i: (i, 0)),
                pl.BlockSpec((K, D), lambda i: (0, 0), pipeline_mode=pl.Buffered(1)),
                row_spec, mod_spec]
    args = [a, w, x2d, mod]
    out_specs = [row_spec]
    out_shape = [jax.ShapeDtypeStruct((T, D), F32)]
    if emit_norm:
        in_specs.append(mod_spec)
        args.append(next_mod)
        out_specs.append(row_spec)
        out_shape.append(jax.ShapeDtypeStruct((T, D), BF16))
    return pl.pallas_call(
        functools.partial(_ffn_down_kernel, emit_norm=emit_norm),
        grid=(T // tm,),
        in_specs=in_specs,
        out_specs=out_specs,
        out_shape=out_shape,
        compiler_params=_params("arbitrary"),
        name="ffn_down_residual_norm",
    )(*args)


def _qkv_kernel(h_ref, w_ref, gn_ref, blk_ref, o_ref, *, n_norm_tiles):
    j = pl.program_id(0)
    acc = _dot(h_ref[...], w_ref[...])

    @pl.when(j < n_norm_tiles)
    def _():
        sq = acc * acc
        hi = sq.astype(BF16)
        lo = (sq - hi.astype(F32)).astype(BF16)
        ss = _dot(hi, blk_ref[...]) + _dot(lo, blk_ref[...])
        o_ref[...] = (acc * lax.rsqrt(ss * (1.0 / SW_HEAD_DIM) + EPS) * gn_ref[...]).astype(BF16)

    @pl.when(j >= n_norm_tiles)
    def _():
        o_ref[...] = acc.astype(BF16)


def _qkv_proj(h, w, gain_row, n_norm_cols, tm=1024, tn=256):
    T, D = h.shape
    N = w.shape[1]
    n_norm_tiles = n_norm_cols // tn
    r = lax.broadcasted_iota(jnp.int32, (tn, tn), 0) // SW_HEAD_DIM
    c = lax.broadcasted_iota(jnp.int32, (tn, tn), 1) // SW_HEAD_DIM
    blk = (r == c).astype(BF16)
    kern = functools.partial(_qkv_kernel, n_norm_tiles=n_norm_tiles)
    return pl.pallas_call(
        kern,
        grid=(N // tn, T // tm),
        in_specs=[pl.BlockSpec((tm, D), lambda j, i: (i, 0)),
                  pl.BlockSpec((D, tn), lambda j, i: (0, j)),
                  pl.BlockSpec((1, tn), lambda j, i: (0, j)),
                  pl.BlockSpec((tn, tn), lambda j, i: (0, 0))],
        out_specs=pl.BlockSpec((tm, tn), lambda j, i: (i, j)),
        out_shape=jax.ShapeDtypeStruct((T, N), BF16),
        compiler_params=_params("arbitrary", "arbitrary"),
        name="swa_qkv_proj",
    )(h, w, gain_row, blk)


def _swa_kernel(q_ref, kp_ref, kc_ref, vp_ref, vc_ref, sink_ref, o_ref):
    blk = pl.program_id(1)
    BLK, hd = SW_BLOCK, SW_HEAD_DIM
    G = SW_HEADS // SW_KV_HEADS
    pairs = G // 2
    upairs = SWA_PAIRS_PER_DOT
    rows, band = upairs * BLK, 2 * BLK
    qi = lax.broadcasted_iota(jnp.int32, (rows, band), 0) & (BLK - 1)
    kr = lax.broadcasted_iota(jnp.int32, (rows, band), 1)
    rel = BLK + qi - kr
    first = jnp.where(blk > 0, 0, BLK)
    bias = jnp.where((rel >= 0) & (rel < BLK) & (kr >= first), 0.0, -jnp.inf)
    lane = lax.broadcasted_iota(jnp.int32, (1, LANES), 1)
    lo = (lane < hd).astype(BF16)
    hi = (lane >= hd).astype(BF16)
    lo_f = lane < hd
    ones_lo = jnp.broadcast_to(lo, (band, LANES))
    ones_hi = jnp.broadcast_to(hi, (band, LANES))
    units = [(kh, u) for kh in range(SW_KV_HEADS) for u in range(pairs // upairs)]

    scores = {}
    for kh in range(SW_KV_HEADS):
        cols = slice(kh * LANES, (kh + 1) * LANES)
        kd = jnp.concatenate([kp_ref[:, cols], kc_ref[:, cols]], axis=0)
        kz = jnp.concatenate([kd * lo, kd * hi], axis=0)
        for u in range(pairs // upairs):
            p0 = kh * pairs + u * upairs
            q = jnp.concatenate([q_ref[:, (p0 + p) * LANES:(p0 + p + 1) * LANES] for p in range(upairs)], axis=0)
            scores[kh, u] = lax.dot_general(q, kz, (((1,), (1,)), ((), ())), preferred_element_type=F32)

    vz = None
    for kh, u in units:
        if u == 0:
            cols = slice(kh * LANES, (kh + 1) * LANES)
            vd = jnp.concatenate([vp_ref[:, cols], vc_ref[:, cols]], axis=0)
            vz = jnp.concatenate([jnp.concatenate([vd * lo, ones_lo], axis=1),
                                  jnp.concatenate([vd * hi, ones_hi], axis=1)], axis=0)
        p0 = kh * pairs + u * upairs
        s = scores[kh, u]
        sink_b = sink_ref[kh, u * rows:(u + 1) * rows, :]
        es, mxs = [], []
        for half in range(2):
            sh = s[:, half * band:(half + 1) * band] + bias
            mx = jnp.maximum(jnp.max(sh, axis=1, keepdims=True), sink_b[:, half * hd:half * hd + 1])
            es.append(jnp.exp(sh - mx).astype(BF16))
            mxs.append(mx)
        o2 = _dot(jnp.concatenate(es, axis=1), vz)
        o = o2[:, :LANES] / (o2[:, LANES:] + jnp.exp(sink_b - jnp.where(lo_f, mxs[0], mxs[1])))
        for p in range(upairs):
            o_ref[:, (p0 + p) * LANES:(p0 + p + 1) * LANES] = o[p * BLK:(p + 1) * BLK].astype(BF16)


def _swa(qkv, sink_cols, batch, seq):
    T = qkv.shape[0]
    BLK = SW_BLOCK
    nb = seq // BLK
    qw = SW_HEADS * SW_HEAD_DIM
    kw = SW_KV_HEADS * LANES
    k_off = qw // kw
    v_off = k_off + 1

    def cur(b, j):
        return b * nb + j

    def prev(b, j):
        return b * nb + jnp.maximum(j - 1, 0)

    return pl.pallas_call(
        _swa_kernel,
        grid=(batch, nb),
        in_specs=[pl.BlockSpec((BLK, qw), lambda b, j: (cur(b, j), 0)),
                  pl.BlockSpec((BLK, kw), lambda b, j: (prev(b, j), k_off)),
                  pl.BlockSpec((BLK, kw), lambda b, j: (cur(b, j), k_off)),
                  pl.BlockSpec((BLK, kw), lambda b, j: (prev(b, j), v_off)),
                  pl.BlockSpec((BLK, kw), lambda b, j: (cur(b, j), v_off)),
                  pl.BlockSpec(sink_cols.shape, lambda b, j: (0, 0, 0))],
        out_specs=pl.BlockSpec((BLK, qw), lambda b, j: (cur(b, j), 0)),
        out_shape=jax.ShapeDtypeStruct((T, qw), BF16),
        compiler_params=_params("arbitrary", "arbitrary"),
        name="swa_attention",
    )(qkv, qkv, qkv, qkv, qkv, sink_cols)


def kernel(x, c, l0_w_mod, l0_b_mod, l0_w_in, l0_gate_bias, l0_h_norm, l0_w_out, l0_w_up, l0_conv_w, l0_conv_b,
           l0_w_down, l1_w_mod, l1_b_mod, l1_w_qkv, l1_q_norm, l1_k_norm, l1_sinks, l1_w_out, l1_w_up, l1_conv_w,
           l1_conv_b, l1_w_down):
    B, S, D = x.shape
    T = B * S
    H, dk, dv = ML_HEADS, ML_DQK, ML_DV
    x2d = x.reshape(T, D)

    mod0 = _modulation(c, l0_w_mod, l0_b_mod).reshape(B, 6, D)
    mod1 = _modulation(c, l1_w_mod, l1_b_mod).reshape(B, 6, D)

    n_main = 2 * H * dk + 2 * H * dv
    w_main = l0_w_in[:, :n_main].astype(BF16)
    w_gate = jnp.pad(l0_w_in[:, n_main:], ((0, 0), (0, LANES - 2 * H))).astype(BF16)
    gate_bias = jnp.pad(l0_gate_bias, (0, LANES - 2 * H)).reshape(1, LANES)
    col_scale = jnp.concatenate([jnp.ones((H * dk,), F32), jnp.full((H * dk,), dk ** -0.5, F32),
                                 jnp.ones((2 * H * dv,), F32)]).reshape(1, n_main)
    hm = _first_norm(x2d, mod0, S)
    qkvo, gcol, grow = _inproj(hm, w_main, col_scale, w_gate, gate_bias)
    y = _mlstm(qkvo, gcol, grow, l0_h_norm.reshape(1, H * dv), B, S)
    x1, hf = _outproj(y, l0_w_out.astype(BF16), x2d, mod0, S)
    act = _ffn_up(hf, l0_w_up.astype(BF16), l0_conv_w, l0_conv_b, S)
    x2, hm1 = _ffn_down(act, l0_w_down.astype(BF16), x1, mod0, mod1, S)

    Hq, Hk, hd = SW_HEADS, SW_KV_HEADS, SW_HEAD_DIM
    scale = hd ** -0.5

    def dup_heads(w):
        w = w.reshape(D, Hk, 1, hd)
        return jnp.concatenate([w, w], axis=2).reshape(D, Hk * 2 * hd)

    w_qkv = jnp.concatenate([l1_w_qkv[:, :Hq * hd], dup_heads(l1_w_qkv[:, Hq * hd:(Hq + Hk) * hd]),
                             dup_heads(l1_w_qkv[:, (Hq + Hk) * hd:])], axis=1).astype(BF16)
    gain = jnp.concatenate([jnp.tile(l1_q_norm * scale, Hq), jnp.tile(l1_k_norm, 2 * Hk),
                            jnp.ones((2 * Hk * hd,), F32)]).reshape(1, -1)
    qkv = _qkv_proj(hm1, w_qkv, gain, n_norm_cols=(Hq + 2 * Hk) * hd)
    pairs = Hq // Hk // 2
    sink_cols = jnp.repeat(jnp.repeat(l1_sinks.reshape(Hk, pairs, 2), SW_BLOCK, axis=1), hd, axis=2)
    att = _swa(qkv, sink_cols, B, S)
    x3, hf1 = _outproj(att, l1_w_out.astype(BF16), x2, mod1, S)
    act1 = _ffn_up(hf1, l1_w_up.astype(BF16), l1_conv_w, l1_conv_b, S)
    (x4,) = _ffn_down(act1, l1_w_down.astype(BF16), x3, mod1, None, S)
    return x4.reshape(B, S, D)
```

```python
import functools
import math

import jax
import jax.numpy as jnp
from jax import lax
from jax.experimental import pallas as pl
from jax.experimental.pallas import tpu as pltpu

EPS = 1e-6
ML_HEADS = 8
ML_DQK = 128
ML_DV = 256
GATE_CAP = 15.0
SW_HEADS = 32
SW_KV_HEADS = 4
SW_HEAD_DIM = 64
SW_BLOCK = 128
CONV_WIDTH = 3
SWA_PAIRS_PER_DOT = 1
FFN_DOT_ROWS = 256
FFN_EPILOGUE_ROWS = 32

LANES = 128
SUBLANES = 8
VMEM_LIMIT = 56 * 1024 * 1024

F32 = jnp.float32
BF16 = jnp.bfloat16


def _params(*sem):
    return pltpu.CompilerParams(dimension_semantics=sem, vmem_limit_bytes=VMEM_LIMIT)


def _dot(a, b):
    return jnp.dot(a, b, preferred_element_type=F32)


def _rms_mod(x, scale_row, shift_row):
    ms = jnp.mean(x * x, axis=-1, keepdims=True)
    return (x * lax.rsqrt(ms + EPS)) * (1.0 + scale_row) + shift_row


def _mod_kernel(c_ref, w_ref, b_ref, o_ref):
    c = c_ref[...]
    sc = (c * jax.nn.sigmoid(c)).astype(BF16)
    o_ref[...] = _dot(sc, w_ref[...].astype(BF16)) + b_ref[...]


def _modulation(c, w_mod, b_mod, tn=1024):
    B, D = c.shape
    N = w_mod.shape[1]
    return pl.pallas_call(
        _mod_kernel,
        grid=(N // tn,),
        in_specs=[pl.BlockSpec((B, D), lambda j: (0, 0)),
                  pl.BlockSpec((D, tn), lambda j: (0, j)),
                  pl.BlockSpec((1, tn), lambda j: (0, j))],
        out_specs=pl.BlockSpec((B, tn), lambda j: (0, j)),
        out_shape=jax.ShapeDtypeStruct((B, N), F32),
        compiler_params=_params("arbitrary"),
        name="modulation",
    )(c, w_mod, b_mod.reshape(1, N))


def _norm_kernel(x_ref, mod_ref, o_ref):
    o_ref[...] = _rms_mod(x_ref[...], mod_ref[0, 1:2, :], mod_ref[0, 0:1, :]).astype(BF16)


def _first_norm(x2d, mod, seq, tm=512):
    T, D = x2d.shape
    per = seq // tm
    return pl.pallas_call(
        _norm_kernel,
        grid=(T // tm,),
        in_specs=[pl.BlockSpec((tm, D), lambda i: (i, 0)),
                  pl.BlockSpec((1, 6, D), lambda i: (i // per, 0, 0))],
        out_specs=pl.BlockSpec((tm, D), lambda i: (i, 0)),
        out_shape=jax.ShapeDtypeStruct((T, D), BF16),
        compiler_params=_params("arbitrary"),
        name="first_norm",
    )(x2d, mod)


def _inproj_kernel(h_ref, w_ref, cs_ref, wg_ref, gb_ref, o_ref, gcol_ref, grow_ref):
    j = pl.program_id(1)
    h = h_ref[...]
    o_ref[...] = (_dot(h, w_ref[...]) * cs_ref[...]).astype(BF16)

    @pl.when(j == 0)
    def _():
        g = _dot(h, wg_ref[...]) + gb_ref[...]
        g = GATE_CAP * jnp.tanh(g * (1.0 / GATE_CAP))
        log_f = jnp.minimum(g, 0.0) - jnp.log(1.0 + jnp.exp(-jnp.abs(g)))
        lane = lax.broadcasted_iota(jnp.int32, g.shape, 1)
        g = jnp.where(lane < ML_HEADS, g, log_f)
        gcol_ref[...] = g
        grow_ref[...] = jnp.transpose(g)[: 2 * ML_HEADS, :]


def _inproj(h, w_main, col_scale, w_gate, gate_bias, tm=2048, tn=1024):
    T, D = h.shape
    N = w_main.shape[1]
    return pl.pallas_call(
        _inproj_kernel,
        grid=(T // tm, N // tn),
        in_specs=[pl.BlockSpec((tm, D), lambda i, j: (i, 0)),
                  pl.BlockSpec((D, tn), lambda i, j: (0, j)),
                  pl.BlockSpec((1, tn), lambda i, j: (0, j)),
                  pl.BlockSpec((D, LANES), lambda i, j: (0, 0)),
                  pl.BlockSpec((1, LANES), lambda i, j: (0, 0))],
        out_specs=[pl.BlockSpec((tm, tn), lambda i, j: (i, j)),
                   pl.BlockSpec((tm, LANES), lambda i, j: (i, 0)),
                   pl.BlockSpec((2 * ML_HEADS, tm), lambda i, j: (0, i))],
        out_shape=[jax.ShapeDtypeStruct((T, N), BF16),
                   jax.ShapeDtypeStruct((T, LANES), F32),
                   jax.ShapeDtypeStruct((2 * ML_HEADS, T), F32)],
        compiler_params=_params("arbitrary", "arbitrary"),
        name="mlstm_inproj",
    )(h, w_main, col_scale, w_gate, gate_bias)


def _mlstm_kernel(q_ref, k_ref, v_ref, o_ref, gcol_ref, grow_ref, hn_ref, y_ref, c_ref, n_ref, m_ref,
                  *, heads_per_step, chunk, seq):
    L = chunk
    hg = heads_per_step
    h0 = pl.program_id(1) * hg
    dk, dv = ML_DQK, ML_DV
    c_ref[...] = jnp.zeros_like(c_ref)
    n_ref[...] = jnp.zeros_like(n_ref)
    m_ref[...] = jnp.zeros_like(m_ref)
    row = lax.broadcasted_iota(jnp.int32, (L, L), 0)
    col = lax.broadcasted_iota(jnp.int32, (L, L), 1)
    causal = col <= row
    lane = lax.broadcasted_iota(jnp.int32, (1, LANES), 1)
    sub = lax.broadcasted_iota(jnp.int32, (2 * ML_HEADS, 1), 0)

    def chunk_step(ci, carry):
        r0 = pl.multiple_of(ci * L, L)
        gcol = gcol_ref[pl.ds(r0, L), :]
        grow = grow_ref[:, pl.ds(r0, L)]
        for hh in range(hg):
            head = h0 + hh
            i_col = jnp.sum(jnp.where(lane == head, gcol, 0.0), axis=1, keepdims=True)
            f_col = jnp.sum(jnp.where(lane == head + ML_HEADS, gcol, 0.0), axis=1, keepdims=True)
            i_row = jnp.sum(jnp.where(sub == head, grow, 0.0), axis=0, keepdims=True)
            f_row = jnp.sum(jnp.where(sub == head + ML_HEADS, grow, 0.0), axis=0, keepdims=True)
            b_col = jnp.sum(jnp.where(causal, f_row, 0.0), axis=1, keepdims=True)
            b_row = jnp.sum(jnp.where(causal, 0.0, f_col), axis=0, keepdims=True) + f_row
            total = jnp.sum(f_row, axis=1, keepdims=True)
            a_row = i_row - b_row
            a_col = i_col - b_col
            m_prev = m_ref[hh]
            dmat = jnp.where(causal, b_col + a_row, -jnp.inf)
            m_inter = b_col + m_prev
            m_t = jnp.maximum(m_inter, jnp.max(dmat, axis=1, keepdims=True))
            w_inter = jnp.exp(m_inter - m_t)
            g_col = total + a_col
            m_new = jnp.maximum(total + m_prev, jnp.max(g_col, axis=0, keepdims=True))
            w_col = jnp.exp(g_col - m_new)
            decay = jnp.exp(total + m_prev - m_new)
            q = q_ref[pl.ds(r0, L), hh * dk:(hh + 1) * dk]
            k = k_ref[pl.ds(r0, L), hh * dk:(hh + 1) * dk]
            v = v_ref[pl.ds(r0, L), hh * dv:(hh + 1) * dv]
            c_state = c_ref[hh]
            n_state = n_ref[hh]
            s = lax.dot_general(q, k, (((1,), (1,)), ((), ())), preferred_element_type=F32)
            p = jnp.exp(dmat - m_t) * s
            num = w_inter * _dot(q, c_state.astype(BF16)) + _dot(p.astype(BF16), v)
            qn = jnp.sum(q.astype(F32) * n_state, axis=1, keepdims=True)
            den = w_inter * qn + jnp.sum(p, axis=1, keepdims=True)
            hc = num / jnp.maximum(jnp.abs(den), jnp.exp(-m_t))
            hs = hc * lax.rsqrt(jnp.mean(hc * hc, axis=1, keepdims=True) + EPS) * hn_ref[:, hh * dv:(hh + 1) * dv]
            og = o_ref[pl.ds(r0, L), hh * dv:(hh + 1) * dv].astype(F32)
            y_ref[pl.ds(r0, L), hh * dv:(hh + 1) * dv] = (jax.nn.sigmoid(og) * hs).astype(BF16)
            k_w = k.astype(F32) * w_col
            c_ref[hh] = decay * c_state + lax.dot_general(
                k_w.astype(BF16), v, (((0,), (0,)), ((), ())), preferred_element_type=F32)
            n_ref[hh] = decay * n_state + jnp.sum(k_w, axis=0, keepdims=True)
            m_ref[hh] = m_new
        return carry

    lax.fori_loop(0, seq // L, chunk_step, 0)


def _mlstm(qkvo, gcol, grow, h_norm, batch, seq, heads_per_step=4, chunk=128):
    T = qkvo.shape[0]
    H, dk, dv = ML_HEADS, ML_DQK, ML_DV
    hg = heads_per_step
    qb, vb = hg * dk, hg * dv
    k_off = (H * dk) // qb
    v_off = (2 * H * dk) // vb
    o_off = (2 * H * dk + H * dv) // vb
    kern = functools.partial(_mlstm_kernel, heads_per_step=hg, chunk=chunk, seq=seq)
    return pl.pallas_call(
        kern,
        grid=(batch, H // hg),
        in_specs=[pl.BlockSpec((seq, qb), lambda b, g: (b, g)),
                  pl.BlockSpec((seq, qb), lambda b, g: (b, k_off + g)),
                  pl.BlockSpec((seq, vb), lambda b, g: (b, v_off + g)),
                  pl.BlockSpec((seq, vb), lambda b, g: (b, o_off + g)),
                  pl.BlockSpec((seq, LANES), lambda b, g: (b, 0)),
                  pl.BlockSpec((2 * H, seq), lambda b, g: (0, b)),
                  pl.BlockSpec((1, vb), lambda b, g: (0, g))],
        out_specs=pl.BlockSpec((seq, vb), lambda b, g: (b, g)),
        out_shape=jax.ShapeDtypeStruct((T, H * dv), BF16),
        scratch_shapes=[pltpu.VMEM((hg, dk, dv), F32),
                        pltpu.VMEM((hg, 1, dk), F32),
                        pltpu.VMEM((hg, 1, 1), F32)],
        compiler_params=_params("arbitrary", "arbitrary"),
        name="mlstm_chunks",
    )(qkvo, qkvo, qkvo, qkvo, gcol, grow, h_norm)


def _outproj_kernel(y_ref, w_ref, x_ref, mod_ref, xo_ref, ho_ref):
    acc = _dot(y_ref[...], w_ref[...])
    x_new = x_ref[...] + mod_ref[0, 2:3, :] * acc
    xo_ref[...] = x_new
    ho_ref[...] = _rms_mod(x_new, mod_ref[0, 4:5, :], mod_ref[0, 3:4, :]).astype(BF16)


def _outproj(y, w, x2d, mod, seq, tm=512):
    T, K = y.shape
    D = w.shape[1]
    per = seq // tm
    return pl.pallas_call(
        _outproj_kernel,
        grid=(T // tm,),
        in_specs=[pl.BlockSpec((tm, K), lambda i: (i, 0)),
                  pl.BlockSpec((K, D), lambda i: (0, 0)),
                  pl.BlockSpec((tm, D), lambda i: (i, 0)),
                  pl.BlockSpec((1, 6, D), lambda i: (i // per, 0, 0))],
        out_specs=[pl.BlockSpec((tm, D), lambda i: (i, 0)),
                   pl.BlockSpec((tm, D), lambda i: (i, 0))],
        out_shape=[jax.ShapeDtypeStruct((T, D), F32),
                   jax.ShapeDtypeStruct((T, D), BF16)],
        compiler_params=_params("arbitrary"),
        name="outproj_residual_norm",
    )(y, w, x2d, mod)


def _ffn_up_kernel(h_ref, wg_ref, wv_ref, cwg_ref, cwv_ref, cbg_ref, cbv_ref, o_ref, ug_ref, uv_ref, *, seq, sub):
    rb, rc = FFN_DOT_ROWS, FFN_EPILOGUE_ROWS
    units = [(s, b0) for s in range(o_ref.shape[1] // sub) for b0 in range(0, seq, rb)]

    def matmuls(n):
        s, b0 = units[n]
        cols = slice(s * sub, (s + 1) * sub)
        h = h_ref[b0:b0 + rb, :]
        for u_ref, w_ref in ((ug_ref, wg_ref), (uv_ref, wv_ref)):
            u_ref[n % 2, :SUBLANES, :] = (u_ref[(n - 1) % 2, rb:, :] if b0 else jnp.zeros((SUBLANES, sub), F32))
            u_ref[n % 2, SUBLANES:, :] = _dot(h, w_ref[:, cols])

    def conv(u_ref, n, r0, cw, cb):
        out = cb + cw[2:3, :] * u_ref[n % 2, SUBLANES + r0:SUBLANES + r0 + rc, :]
        out = out + cw[1:2, :] * u_ref[n % 2, SUBLANES - 1 + r0:SUBLANES - 1 + r0 + rc, :]
        out = out + cw[0:1, :] * u_ref[n % 2, SUBLANES - 2 + r0:SUBLANES - 2 + r0 + rc, :]
        return out

    def epilogue(n):
        s, b0 = units[n]
        cols = slice(s * sub, (s + 1) * sub)
        cwg, cwv, cbg, cbv = cwg_ref[:, cols], cwv_ref[:, cols], cbg_ref[:, cols], cbv_ref[:, cols]
        for r0 in range(0, rb, rc):
            g = conv(ug_ref, n, r0, cwg, cbg)
            v = conv(uv_ref, n, r0, cwv, cbv)
            o_ref[b0 + r0:b0 + r0 + rc, cols] = (g * jax.nn.sigmoid(g) * v).astype(BF16)

    for n in range(len(units)):
        matmuls(n)
        if n:
            epilogue(n - 1)
    epilogue(len(units) - 1)


def _ffn_up(h, w_up, conv_w, conv_b, seq, tn=512, sub=256):
    T, D = h.shape
    F = w_up.shape[1] // 2
    nf = F // tn
    kern = functools.partial(_ffn_up_kernel, seq=seq, sub=sub)
    cb = conv_b.reshape(1, 2 * F)
    return pl.pallas_call(
        kern,
        grid=(T // seq, nf),
        in_specs=[pl.BlockSpec((seq, D), lambda i, j: (i, 0)),
                  pl.BlockSpec((D, tn), lambda i, j: (0, j)),
                  pl.BlockSpec((D, tn), lambda i, j: (0, nf + j)),
                  pl.BlockSpec((CONV_WIDTH, tn), lambda i, j: (0, j)),
                  pl.BlockSpec((CONV_WIDTH, tn), lambda i, j: (0, nf + j)),
                  pl.BlockSpec((1, tn), lambda i, j: (0, j)),
                  pl.BlockSpec((1, tn), lambda i, j: (0, nf + j))],
        out_specs=pl.BlockSpec((seq, tn), lambda i, j: (i, j)),
        out_shape=jax.ShapeDtypeStruct((T, F), BF16),
        scratch_shapes=[pltpu.VMEM((2, SUBLANES + FFN_DOT_ROWS, sub), F32),
                        pltpu.VMEM((2, SUBLANES + FFN_DOT_ROWS, sub), F32)],
        compiler_params=_params("arbitrary", "arbitrary"),
        name="ffn_up_conv_act",
    )(h, w_up, w_up, conv_w, conv_w, cb, cb)


def _ffn_down_kernel(a_ref, w_ref, x_ref, mod_ref, *rest, emit_norm):
    x_new = x_ref[...] + mod_ref[0, 5:6, :] * _dot(a_ref[...], w_ref[...])
    if emit_norm:
        nmod_ref, xo_ref, ho_ref = rest
        ho_ref[...] = _rms_mod(x_new, nmod_ref[0, 1:2, :], nmod_ref[0, 0:1, :]).astype(BF16)
    else:
        (xo_ref,) = rest
    xo_ref[...] = x_new


def _ffn_down(a, w, x2d, mod, next_mod, seq, tm=256):
    T, K = a.shape
    D = w.shape[1]
    per = seq // tm
    emit_norm = next_mod is not None
    mod_spec = pl.BlockSpec((1, 6, D), lambda i: (i // per, 0, 0))
    row_spec = pl.BlockSpec((tm, D), lambda i: (i, 0))
    in_specs = [pl.BlockSpec((tm, K), lambda i: (i, 0)),
                pl.BlockSpec((K, D), lambda i: (0, 0), pipeline_mode=pl.Buffered(1)),
                row_spec, mod_spec]
    args = [a, w, x2d, mod]
    out_specs = [row_spec]
    out_shape = [jax.ShapeDtypeStruct((T, D), F32)]
    if emit_norm:
        in_specs.append(mod_spec)
        args.append(next_mod)
        out_specs.append(row_spec)
        out_shape.append(jax.ShapeDtypeStruct((T, D), BF16))
    return pl.pallas_call(
        functools.partial(_ffn_down_kernel, emit_norm=emit_norm),
        grid=(T // tm,),
        in_specs=in_specs,
        out_specs=out_specs,
        out_shape=out_shape,
        compiler_params=_params("arbitrary"),
        name="ffn_down_residual_norm",
    )(*args)


def _qkv_kernel(h_ref, w_ref, gn_ref, blk_ref, o_ref, *, n_norm_pieces, sub):
    j = pl.program_id(1)
    h = h_ref[...]
    tn = o_ref.shape[1]
    pieces = [slice(s, s + sub) for s in range(0, tn, sub)]

    def tile(n_norm_pieces):
        accs = [_dot(h, w_ref[:, cols]) for cols in pieces]
        for p, cols in enumerate(pieces):
            acc = accs[p]
            if p < n_norm_pieces:
                ss = _dot((acc * acc).astype(BF16), blk_ref[...])
                acc = acc * lax.rsqrt(ss * (1.0 / SW_HEAD_DIM) + EPS) * gn_ref[:, cols]
            o_ref[:, cols] = acc.astype(BF16)

    full, rest = divmod(n_norm_pieces, len(pieces))

    @pl.when(j < full)
    def _():
        tile(len(pieces))

    @pl.when(j == full)
    def _():
        tile(rest)

    @pl.when(j > full)
    def _():
        tile(0)


def _qkv_proj(h, w, gain_row, n_norm_cols, tm=2048, tn=1024, sub=256):
    T, D = h.shape
    N = w.shape[1]
    r = lax.broadcasted_iota(jnp.int32, (sub, sub), 0) // SW_HEAD_DIM
    c = lax.broadcasted_iota(jnp.int32, (sub, sub), 1) // SW_HEAD_DIM
    blk = (r == c).astype(BF16)
    kern = functools.partial(_qkv_kernel, n_norm_pieces=n_norm_cols // sub, sub=sub)
    return pl.pallas_call(
        kern,
        grid=(T // tm, N // tn),
        in_specs=[pl.BlockSpec((tm, D), lambda i, j: (i, 0)),
                  pl.BlockSpec((D, tn), lambda i, j: (0, j)),
                  pl.BlockSpec((1, tn), lambda i, j: (0, j)),
                  pl.BlockSpec((sub, sub), lambda i, j: (0, 0))],
        out_specs=pl.BlockSpec((tm, tn), lambda i, j: (i, j)),
        out_shape=jax.ShapeDtypeStruct((T, N), BF16),
        compiler_params=_params("arbitrary", "arbitrary"),
        name="swa_qkv_proj",
    )(h, w, gain_row, blk)


def _swa_kernel(q_ref, kp_ref, kc_ref, vp_ref, vc_ref, sink_ref, o_ref):
    blk = pl.program_id(1)
    BLK, hd = SW_BLOCK, SW_HEAD_DIM
    G = SW_HEADS // SW_KV_HEADS
    pairs = G // 2
    upairs = SWA_PAIRS_PER_DOT
    rows, band = upairs * BLK, 2 * BLK
    qi = lax.broadcasted_iota(jnp.int32, (rows, band), 0) & (BLK - 1)
    kr = lax.broadcasted_iota(jnp.int32, (rows, band), 1)
    rel = BLK + qi - kr
    first = jnp.where(blk > 0, 0, BLK)
    bias = jnp.where((rel >= 0) & (rel < BLK) & (kr >= first), 0.0, -jnp.inf)
    lane = lax.broadcasted_iota(jnp.int32, (1, LANES), 1)
    lo = (lane < hd).astype(BF16)
    hi = (lane >= hd).astype(BF16)
    lo_f = lane < hd
    ones_lo = jnp.broadcast_to(lo, (band, LANES))
    ones_hi = jnp.broadcast_to(hi, (band, LANES))
    units = [(kh, u) for kh in range(SW_KV_HEADS) for u in range(pairs // upairs)]

    scores = {}
    for kh in range(SW_KV_HEADS):
        cols = slice(kh * LANES, (kh + 1) * LANES)
        kd = jnp.concatenate([kp_ref[:, cols], kc_ref[:, cols]], axis=0)
        kz = jnp.concatenate([kd * lo, kd * hi], axis=0)
        for u in range(pairs // upairs):
            p0 = kh * pairs + u * upairs
            q = jnp.concatenate([q_ref[:, (p0 + p) * LANES:(p0 + p + 1) * LANES] for p in range(upairs)], axis=0)
            scores[kh, u] = lax.dot_general(q, kz, (((1,), (1,)), ((), ())), preferred_element_type=F32)

    vz = None
    for kh, u in units:
        if u == 0:
            cols = slice(kh * LANES, (kh + 1) * LANES)
            vd = jnp.concatenate([vp_ref[:, cols], vc_ref[:, cols]], axis=0)
            vz = jnp.concatenate([jnp.concatenate([vd * lo, ones_lo], axis=1),
                                  jnp.concatenate([vd * hi, ones_hi], axis=1)], axis=0)
        p0 = kh * pairs + u * upairs
        s = scores[kh, u]
        sink_b = sink_ref[kh, u * rows:(u + 1) * rows, :]
        es, mxs = [], []
        for half in range(2):
            sh = s[:, half * band:(half + 1) * band] + bias
            mx = jnp.maximum(jnp.max(sh, axis=1, keepdims=True), sink_b[:, half * hd:half * hd + 1])
            es.append(jnp.exp(sh - mx).astype(BF16))
            mxs.append(mx)
        o2 = _dot(jnp.concatenate(es, axis=1), vz)
        o = o2[:, :LANES] / (o2[:, LANES:] + jnp.exp(sink_b - jnp.where(lo_f, mxs[0], mxs[1])))
        for p in range(upairs):
            o_ref[:, (p0 + p) * LANES:(p0 + p + 1) * LANES] = o[p * BLK:(p + 1) * BLK].astype(BF16)


def _swa(qkv, sink_cols, batch, seq):
    T = qkv.shape[0]
    BLK = SW_BLOCK
    nb = seq // BLK
    qw = SW_HEADS * SW_HEAD_DIM
    kw = SW_KV_HEADS * LANES
    k_off = qw // kw
    v_off = k_off + 1

    def cur(b, j):
        return b * nb + j

    def prev(b, j):
        return b * nb + jnp.maximum(j - 1, 0)

    return pl.pallas_call(
        _swa_kernel,
        grid=(batch, nb),
        in_specs=[pl.BlockSpec((BLK, qw), lambda b, j: (cur(b, j), 0)),
                  pl.BlockSpec((BLK, kw), lambda b, j: (prev(b, j), k_off)),
                  pl.BlockSpec((BLK, kw), lambda b, j: (cur(b, j), k_off)),
                  pl.BlockSpec((BLK, kw), lambda b, j: (prev(b, j), v_off)),
                  pl.BlockSpec((BLK, kw), lambda b, j: (cur(b, j), v_off)),
                  pl.BlockSpec(sink_cols.shape, lambda b, j: (0, 0, 0))],
        out_specs=pl.BlockSpec((BLK, qw), lambda b, j: (cur(b, j), 0)),
        out_shape=jax.ShapeDtypeStruct((T, qw), BF16),
        compiler_params=_params("arbitrary", "arbitrary"),
        name="swa_attention",
    )(qkv, qkv, qkv, qkv, qkv, sink_cols)


def kernel(x, c, l0_w_mod, l0_b_mod, l0_w_in, l0_gate_bias, l0_h_norm, l0_w_out, l0_w_up, l0_conv_w, l0_conv_b,
           l0_w_down, l1_w_mod, l1_b_mod, l1_w_qkv, l1_q_norm, l1_k_norm, l1_sinks, l1_w_out, l1_w_up, l1_conv_w,
           l1_conv_b, l1_w_down):
    B, S, D = x.shape
    T = B * S
    H, dk, dv = ML_HEADS, ML_DQK, ML_DV
    x2d = x.reshape(T, D)

    mod0 = _modulation(c, l0_w_mod, l0_b_mod).reshape(B, 6, D)
    mod1 = _modulation(c, l1_w_mod, l1_b_mod).reshape(B, 6, D)

    n_main = 2 * H * dk + 2 * H * dv
    w_main = l0_w_in[:, :n_main].astype(BF16)
    w_gate = jnp.pad(l0_w_in[:, n_main:], ((0, 0), (0, LANES - 2 * H))).astype(BF16)
    gate_bias = jnp.pad(l0_gate_bias, (0, LANES - 2 * H)).reshape(1, LANES)
    col_scale = jnp.concatenate([jnp.ones((H * dk,), F32), jnp.full((H * dk,), dk ** -0.5, F32),
                                 jnp.ones((2 * H * dv,), F32)]).reshape(1, n_main)
    hm = _first_norm(x2d, mod0, S)
    qkvo, gcol, grow = _inproj(hm, w_main, col_scale, w_gate, gate_bias)
    y = _mlstm(qkvo, gcol, grow, l0_h_norm.reshape(1, H * dv), B, S)
    x1, hf = _outproj(y, l0_w_out.astype(BF16), x2d, mod0, S)
    act = _ffn_up(hf, l0_w_up.astype(BF16), l0_conv_w, l0_conv_b, S)
    x2, hm1 = _ffn_down(act, l0_w_down.astype(BF16), x1, mod0, mod1, S)

    Hq, Hk, hd = SW_HEADS, SW_KV_HEADS, SW_HEAD_DIM
    scale = hd ** -0.5

    def dup_heads(w):
        w = w.reshape(D, Hk, 1, hd)
        return jnp.concatenate([w, w], axis=2).reshape(D, Hk * 2 * hd)

    w_qkv = jnp.concatenate([l1_w_qkv[:, :Hq * hd], dup_heads(l1_w_qkv[:, Hq * hd:(Hq + Hk) * hd]),
                             dup_heads(l1_w_qkv[:, (Hq + Hk) * hd:])], axis=1).astype(BF16)
    gain = jnp.concatenate([jnp.tile(l1_q_norm * scale, Hq), jnp.tile(l1_k_norm, 2 * Hk),
                            jnp.ones((2 * Hk * hd,), F32)]).reshape(1, -1)
    qkv = _qkv_proj(hm1, w_qkv, gain, n_norm_cols=(Hq + 2 * Hk) * hd)
    pairs = Hq // Hk // 2
    sink_cols = jnp.repeat(jnp.repeat(l1_sinks.reshape(Hk, pairs, 2), SW_BLOCK, axis=1), hd, axis=2)
    att = _swa(qkv, sink_cols, B, S)
    x3, hf1 = _outproj(att, l1_w_out.astype(BF16), x2, mod1, S)
    act1 = _ffn_up(hf1, l1_w_up.astype(BF16), l1_conv_w, l1_conv_b, S)
    (x4,) = _ffn_down(act1, l1_w_down.astype(BF16), x3, mod1, None, S)
    return x4.reshape(B, S, D)
```

```python
import functools
import math

import jax
import jax.numpy as jnp
from jax import lax
from jax.experimental import pallas as pl
from jax.experimental.pallas import tpu as pltpu

EPS = 1e-6
ML_HEADS = 8
ML_DQK = 128
ML_DV = 256
GATE_CAP = 15.0
SW_HEADS = 32
SW_KV_HEADS = 4
SW_HEAD_DIM = 64
SW_BLOCK = 128
CONV_WIDTH = 3
SWA_PAIRS_PER_DOT = 1
FFN_DOT_ROWS = 256
FFN_EPILOGUE_ROWS = 32

LANES = 128
SUBLANES = 8
VMEM_LIMIT = 56 * 1024 * 1024

F32 = jnp.float32
BF16 = jnp.bfloat16


def _params(*sem):
    return pltpu.CompilerParams(dimension_semantics=sem, vmem_limit_bytes=VMEM_LIMIT)


def _dot(a, b):
    return jnp.dot(a, b, preferred_element_type=F32)


def _rms_mod(x, scale_row, shift_row):
    ms = jnp.mean(x * x, axis=-1, keepdims=True)
    return (x * lax.rsqrt(ms + EPS)) * (1.0 + scale_row) + shift_row


def _mod_kernel(c_ref, w_ref, b_ref, o_ref):
    c = c_ref[...]
    sc = (c * jax.nn.sigmoid(c)).astype(BF16)
    o_ref[...] = _dot(sc, w_ref[...].astype(BF16)) + b_ref[...]


def _modulation(c, w_mod, b_mod, tn=1024):
    B, D = c.shape
    N = w_mod.shape[1]
    return pl.pallas_call(
        _mod_kernel,
        grid=(N // tn,),
        in_specs=[pl.BlockSpec((B, D), lambda j: (0, 0)),
                  pl.BlockSpec((D, tn), lambda j: (0, j)),
                  pl.BlockSpec((1, tn), lambda j: (0, j))],
        out_specs=pl.BlockSpec((B, tn), lambda j: (0, j)),
        out_shape=jax.ShapeDtypeStruct((B, N), F32),
        compiler_params=_params("arbitrary"),
        name="modulation",
    )(c, w_mod, b_mod.reshape(1, N))


def _norm_kernel(x_ref, mod_ref, o_ref):
    o_ref[...] = _rms_mod(x_ref[...], mod_ref[0, 1:2, :], mod_ref[0, 0:1, :]).astype(BF16)


def _first_norm(x2d, mod, seq, tm=512):
    T, D = x2d.shape
    per = seq // tm
    return pl.pallas_call(
        _norm_kernel,
        grid=(T // tm,),
        in_specs=[pl.BlockSpec((tm, D), lambda i: (i, 0)),
                  pl.BlockSpec((1, 6, D), lambda i: (i // per, 0, 0))],
        out_specs=pl.BlockSpec((tm, D), lambda i: (i, 0)),
        out_shape=jax.ShapeDtypeStruct((T, D), BF16),
        compiler_params=_params("arbitrary"),
        name="first_norm",
    )(x2d, mod)


def _inproj_kernel(h_ref, w_ref, cs_ref, wg_ref, gb_ref, o_ref, gcol_ref, grow_ref):
    j = pl.program_id(1)
    h = h_ref[...]
    o_ref[...] = (_dot(h, w_ref[...]) * cs_ref[...]).astype(BF16)

    @pl.when(j == 0)
    def _():
        g = _dot(h, wg_ref[...]) + gb_ref[...]
        g = GATE_CAP * jnp.tanh(g * (1.0 / GATE_CAP))
        log_f = jnp.minimum(g, 0.0) - jnp.log(1.0 + jnp.exp(-jnp.abs(g)))
        lane = lax.broadcasted_iota(jnp.int32, g.shape, 1)
        g = jnp.where(lane < ML_HEADS, g, log_f)
        gcol_ref[...] = g
        grow_ref[...] = jnp.transpose(g)[: 2 * ML_HEADS, :]


def _inproj(h, w_main, col_scale, w_gate, gate_bias, tm=2048, tn=1024):
    T, D = h.shape
    N = col_scale.shape[1]
    return pl.pallas_call(
        _inproj_kernel,
        grid=(T // tm, N // tn),
        in_specs=[pl.BlockSpec((tm, D), lambda i, j: (i, 0)),
                  pl.BlockSpec((D, tn), lambda i, j: (0, j)),
                  pl.BlockSpec((1, tn), lambda i, j: (0, j)),
                  pl.BlockSpec((D, LANES), lambda i, j: (0, 0)),
                  pl.BlockSpec((1, LANES), lambda i, j: (0, 0))],
        out_specs=[pl.BlockSpec((tm, tn), lambda i, j: (i, j)),
                   pl.BlockSpec((tm, LANES), lambda i, j: (i, 0)),
                   pl.BlockSpec((2 * ML_HEADS, tm), lambda i, j: (0, i))],
        out_shape=[jax.ShapeDtypeStruct((T, N), BF16),
                   jax.ShapeDtypeStruct((T, LANES), F32),
                   jax.ShapeDtypeStruct((2 * ML_HEADS, T), F32)],
        compiler_params=_params("arbitrary", "arbitrary"),
        name="mlstm_inproj",
    )(h, w_main, col_scale, w_gate, gate_bias)


def _split3(x):
    hi = x.astype(BF16)
    r1 = x - hi.astype(F32)
    mid = r1.astype(BF16)
    lo = (r1 - mid.astype(F32)).astype(BF16)
    return hi, mid, lo


def _mlstm_kernel(q_ref, k_ref, v_ref, o_ref, gcol_ref, grow_ref, hn_ref, y_ref,
                  dexp_ref, wint_ref, floor_ref, wcol_ref, decay_ref, cn_ref,
                  *, heads_per_step, seq):
    L = LANES
    hg = heads_per_step
    nc = seq // L
    dk, dv = ML_DQK, ML_DV
    h0 = pl.program_id(1) * hg
    row = lax.broadcasted_iota(jnp.int32, (L, L), 0)
    col = lax.broadcasted_iota(jnp.int32, (L, L), 1)
    causal = col <= row
    tril = jnp.where(causal, 1.0, 0.0).astype(BF16)
    triu = jnp.where(row <= col, 1.0, 0.0).astype(BF16)

    sel_r = lax.broadcasted_iota(jnp.int32, (L, hg * L), 0)
    sel_head = h0 + lax.broadcasted_iota(jnp.int32, (L, hg * L), 1) // L
    sel_i = jnp.where(sel_r == sel_head, 1.0, 0.0).astype(BF16)
    sel_f = jnp.where(sel_r == sel_head + ML_HEADS, 1.0, 0.0).astype(BF16)
    gate_row = lax.broadcasted_iota(jnp.int32, (2 * ML_HEADS, 1), 0)
    m_run = [jnp.zeros((1, L), F32) for _ in range(hg)]
    spread = []
    for c in range(nc):
        rows = slice(c * L, (c + 1) * L)
        terms = _split3(gcol_ref[rows, :])
        ib_all = sum(_dot(t, sel_i) for t in terms)
        f_terms = [_dot(t, sel_f).astype(BF16) for t in terms]
        grow = grow_ref[:, rows]
        cgrow = sum(_dot(t, triu) for t in _split3(grow))
        spread.append((ib_all, f_terms, grow, cgrow))
    for c in range(nc):
        ib_all, f_terms, grow, cgrow = spread[c]
        bb_all = sum(_dot(tril, t) for t in f_terms)
        for hh in range(hg):
            head = h0 + hh
            m_c = m_run[hh]
            ib, bb = ib_all[:, hh * L:(hh + 1) * L], bb_all[:, hh * L:(hh + 1) * L]
            a_row = jnp.sum(jnp.where(gate_row == head, grow, 0.0)
                            - jnp.where(gate_row == head + ML_HEADS, cgrow, 0.0),
                            axis=0, keepdims=True)
            ca = jnp.max(jnp.where(causal, a_row, -jnp.inf), axis=1, keepdims=True)
            mm = jnp.maximum(m_c, jnp.broadcast_to(ca, (L, L)))
            total = bb[L - 1:L, :]
            m_next = total + mm[L - 1:L, :]
            dexp_ref[hh, c] = jnp.exp(jnp.where(causal, a_row - mm, -jnp.inf))
            wint_ref[hh, c] = jnp.exp(m_c - mm)
            floor_ref[hh, c] = jnp.exp(-(bb + mm))
            wcol_ref[hh, c] = jnp.exp(total + (ib - bb) - m_next)
            decay_ref[hh, c] = jnp.broadcast_to(jnp.exp(total + m_c - m_next), (SUBLANES, L))
            m_run[hh] = m_next

    cn_ref[...] = jnp.zeros_like(cn_ref)
    ones = jnp.ones((L, L), BF16)

    def chunk_step(ci, carry):
        r0 = pl.multiple_of(ci * L, L)
        early = []
        for hh in range(hg):
            q = q_ref[pl.ds(r0, L), hh * dk:(hh + 1) * dk]
            k = k_ref[pl.ds(r0, L), hh * dk:(hh + 1) * dk]
            v = v_ref[pl.ds(r0, L), hh * dv:(hh + 1) * dv]
            v_ext = jnp.concatenate([v, ones], axis=1)
            cn = cn_ref[hh]
            k_w = (k.astype(F32) * wcol_ref[hh, ci]).astype(BF16)
            kv = lax.dot_general(k_w, v_ext, (((0,), (0,)), ((), ())), preferred_element_type=F32)
            s = lax.dot_general(q, k, (((1,), (1,)), ((), ())), preferred_element_type=F32)
            inter = _dot(q, cn.astype(BF16))
            dec = decay_ref[hh, ci][0:1, :]
            cn_ref[hh] = jnp.concatenate([dec] * (cn.shape[1] // L), axis=1) * cn + kv
            early.append((s, inter, v_ext))
        for hh in range(hg):
            s, inter, v_ext = early[hh]
            p = (dexp_ref[hh, ci] * s).astype(BF16)
            intra = _dot(p, v_ext)
            wint = wint_ref[hh, ci]
            num = jnp.concatenate([wint, wint], axis=1) * inter[:, :dv] + intra[:, :dv]
            den = wint * inter[:, dv:] + intra[:, dv:]
            rdn = 1.0 / jnp.maximum(jnp.abs(den), floor_ref[hh, ci])
            hc = num * jnp.concatenate([rdn, rdn], axis=1)
            hs = hc * lax.rsqrt(jnp.mean(hc * hc, axis=1, keepdims=True) + EPS) * hn_ref[:, hh * dv:(hh + 1) * dv]
            og = o_ref[pl.ds(r0, L), hh * dv:(hh + 1) * dv].astype(F32)
            y_ref[pl.ds(r0, L), hh * dv:(hh + 1) * dv] = (jax.nn.sigmoid(og) * hs).astype(BF16)
        return carry

    lax.fori_loop(0, nc, chunk_step, 0, unroll=4)


def _mlstm(qkvo, gcol, grow, h_norm, batch, seq, heads_per_step=2):
    T = qkvo.shape[0]
    H, dk, dv = ML_HEADS, ML_DQK, ML_DV
    assert dk == LANES and seq % LANES == 0
    hg = heads_per_step
    nc = seq // LANES
    qb, vb = hg * dk, hg * dv
    k_off = (H * dk) // qb
    v_off = (2 * H * dk) // vb
    o_off = (2 * H * dk + H * dv) // vb
    kern = functools.partial(_mlstm_kernel, heads_per_step=hg, seq=seq)
    factor = pltpu.VMEM((hg, nc, LANES, LANES), F32)
    return pl.pallas_call(
        kern,
        grid=(batch, H // hg),
        in_specs=[pl.BlockSpec((seq, qb), lambda b, g: (b, g)),
                  pl.BlockSpec((seq, qb), lambda b, g: (b, k_off + g)),
                  pl.BlockSpec((seq, vb), lambda b, g: (b, v_off + g)),
                  pl.BlockSpec((seq, vb), lambda b, g: (b, o_off + g)),
                  pl.BlockSpec((seq, LANES), lambda b, g: (b, 0)),
                  pl.BlockSpec((2 * H, seq), lambda b, g: (0, b)),
                  pl.BlockSpec((1, vb), lambda b, g: (0, g))],
        out_specs=pl.BlockSpec((seq, vb), lambda b, g: (b, g)),
        out_shape=jax.ShapeDtypeStruct((T, H * dv), BF16),
        scratch_shapes=[factor, factor, factor, factor,
                        pltpu.VMEM((hg, nc, SUBLANES, LANES), F32),
                        pltpu.VMEM((hg, dk, dv + LANES), F32)],
        compiler_params=_params("arbitrary", "arbitrary"),
        name="mlstm_chunks",
    )(qkvo, qkvo, qkvo, qkvo, gcol, grow, h_norm)


def _outproj_kernel(y_ref, w_ref, x_ref, mod_ref, xo_ref, ho_ref):
    acc = _dot(y_ref[...], w_ref[...])
    x_new = x_ref[...] + mod_ref[0, 2:3, :] * acc
    xo_ref[...] = x_new
    ho_ref[...] = _rms_mod(x_new, mod_ref[0, 4:5, :], mod_ref[0, 3:4, :]).astype(BF16)


def _outproj(y, w, x2d, mod, seq, tm=512):
    T, K = y.shape
    D = w.shape[1]
    per = seq // tm
    return pl.pallas_call(
        _outproj_kernel,
        grid=(T // tm,),
        in_specs=[pl.BlockSpec((tm, K), lambda i: (i, 0)),
                  pl.BlockSpec((K, D), lambda i: (0, 0)),
                  pl.BlockSpec((tm, D), lambda i: (i, 0)),
                  pl.BlockSpec((1, 6, D), lambda i: (i // per, 0, 0))],
        out_specs=[pl.BlockSpec((tm, D), lambda i: (i, 0)),
                   pl.BlockSpec((tm, D), lambda i: (i, 0))],
        out_shape=[jax.ShapeDtypeStruct((T, D), F32),
                   jax.ShapeDtypeStruct((T, D), BF16)],
        compiler_params=_params("arbitrary"),
        name="outproj_residual_norm",
    )(y, w, x2d, mod)


def _ffn_up_kernel(h_ref, wg_ref, wv_ref, cwg_ref, cwv_ref, cbg_ref, cbv_ref, o_ref, ug_ref, uv_ref, *, seq, sub):
    rb, rc = FFN_DOT_ROWS, FFN_EPILOGUE_ROWS
    units = [(s, b0) for s in range(o_ref.shape[1] // sub) for b0 in range(0, seq, rb)]

    def matmuls(n):
        s, b0 = units[n]
        cols = slice(s * sub, (s + 1) * sub)
        h = h_ref[b0:b0 + rb, :]
        for u_ref, w_ref in ((ug_ref, wg_ref), (uv_ref, wv_ref)):
            u_ref[n % 2, :SUBLANES, :] = (u_ref[(n - 1) % 2, rb:, :] if b0 else jnp.zeros((SUBLANES, sub), F32))
            u_ref[n % 2, SUBLANES:, :] = _dot(h, w_ref[:, cols])

    def conv(u_ref, n, r0, cw, cb):
        out = cb + cw[2:3, :] * u_ref[n % 2, SUBLANES + r0:SUBLANES + r0 + rc, :]
        out = out + cw[1:2, :] * u_ref[n % 2, SUBLANES - 1 + r0:SUBLANES - 1 + r0 + rc, :]
        out = out + cw[0:1, :] * u_ref[n % 2, SUBLANES - 2 + r0:SUBLANES - 2 + r0 + rc, :]
        return out

    def epilogue(n):
        s, b0 = units[n]
        cols = slice(s * sub, (s + 1) * sub)
        cwg, cwv, cbg, cbv = cwg_ref[:, cols], cwv_ref[:, cols], cbg_ref[:, cols], cbv_ref[:, cols]
        for r0 in range(0, rb, rc):
            g = conv(ug_ref, n, r0, cwg, cbg)
            v = conv(uv_ref, n, r0, cwv, cbv)
            o_ref[b0 + r0:b0 + r0 + rc, cols] = (g * jax.nn.sigmoid(g) * v).astype(BF16)

    for n in range(len(units)):
        matmuls(n)
        if n:
            epilogue(n - 1)
    epilogue(len(units) - 1)


def _ffn_up(h, w_up, conv_w, conv_b, seq, tn=512, sub=256):
    T, D = h.shape
    F = w_up.shape[1] // 2
    nf = F // tn
    kern = functools.partial(_ffn_up_kernel, seq=seq, sub=sub)
    cb = conv_b.reshape(1, 2 * F)
    return pl.pallas_call(
        kern,
        grid=(T // seq, nf),
        in_specs=[pl.BlockSpec((seq, D), lambda i, j: (i, 0)),
                  pl.BlockSpec((D, tn), lambda i, j: (0, j)),
                  pl.BlockSpec((D, tn), lambda i, j: (0, nf + j)),
                  pl.BlockSpec((CONV_WIDTH, tn), lambda i, j: (0, j)),
                  pl.BlockSpec((CONV_WIDTH, tn), lambda i, j: (0, nf + j)),
                  pl.BlockSpec((1, tn), lambda i, j: (0, j)),
                  pl.BlockSpec((1, tn), lambda i, j: (0, nf + j))],
        out_specs=pl.BlockSpec((seq, tn), lambda i, j: (i, j)),
        out_shape=jax.ShapeDtypeStruct((T, F), BF16),
        scratch_shapes=[pltpu.VMEM((2, SUBLANES + FFN_DOT_ROWS, sub), F32),
                        pltpu.VMEM((2, SUBLANES + FFN_DOT_ROWS, sub), F32)],
        compiler_params=_params("arbitrary", "arbitrary"),
        name="ffn_up_conv_act",
    )(h, w_up, w_up, conv_w, conv_w, cb, cb)


def _ffn_down_kernel(a_ref, w_ref, x_ref, mod_ref, *rest, emit_norm):
    x_new = x_ref[...] + mod_ref[0, 5:6, :] * _dot(a_ref[...], w_ref[...])
    if emit_norm:
        nmod_ref, xo_ref, ho_ref = rest
        ho_ref[...] = _rms_mod(x_new, nmod_ref[0, 1:2, :], nmod_ref[0, 0:1, :]).astype(BF16)
    else:
        (xo_ref,) = rest
    xo_ref[...] = x_new


def _ffn_down(a, w, x2d, mod, next_mod, seq, tm=256):
    T, K = a.shape
    D = w.shape[1]
    per = seq // tm
    emit_norm = next_mod is not None
    mod_spec = pl.BlockSpec((1, 6, D), lambda i: (i // per, 0, 0))
    row_spec = pl.BlockSpec((tm, D), lambda i: (i, 0))
    in_specs = [pl.BlockSpec((tm, K), lambda i: (i, 0)),
                pl.BlockSpec((K, D), lambda i: (0, 0), pipeline_mode=pl.Buffered(1)),
                row_spec, mod_spec]
    args = [a, w, x2d, mod]
    out_specs = [row_spec]
    out_shape = [jax.ShapeDtypeStruct((T, D), F32)]
    if emit_norm:
        in_specs.append(mod_spec)
        args.append(next_mod)
        out_specs.append(row_spec)
        out_shape.append(jax.ShapeDtypeStruct((T, D), BF16))
    return pl.pallas_call(
        functools.partial(_ffn_down_kernel, emit_norm=emit_norm),
        grid=(T // tm,),
        in_specs=in_specs,
        out_specs=out_specs,
        out_shape=out_shape,
        compiler_params=_params("arbitrary"),
        name="ffn_down_residual_norm",
    )(*args)


def _qkv_kernel(h_ref, w_ref, gn_ref, blk_ref, o_ref, *, n_norm_pieces, sub):
    j = pl.program_id(1)
    h = h_ref[...]
    tn = o_ref.shape[1]
    pieces = [slice(s, s + sub) for s in range(0, tn, sub)]

    def tile(n_norm_pieces):
        accs = [_dot(h, w_ref[:, cols]) for cols in pieces]
        for p, cols in enumerate(pieces):
            acc = accs[p]
            if p < n_norm_pieces:
                ss = _dot((acc * acc).astype(BF16), blk_ref[...])
                acc = acc * lax.rsqrt(ss * (1.0 / SW_HEAD_DIM) + EPS) * gn_ref[:, cols]
            o_ref[:, cols] = acc.astype(BF16)

    full, rest = divmod(n_norm_pieces, len(pieces))

    @pl.when(j < full)
    def _():
        tile(len(pieces))

    @pl.when(j == full)
    def _():
        tile(rest)

    @pl.when(j > full)
    def _():
        tile(0)


def _qkv_proj(h, w, gain_row, n_norm_cols, tm=2048, tn=1024, sub=256):
    T, D = h.shape
    N = w.shape[1]
    r = lax.broadcasted_iota(jnp.int32, (sub, sub), 0) // SW_HEAD_DIM
    c = lax.broadcasted_iota(jnp.int32, (sub, sub), 1) // SW_HEAD_DIM
    blk = (r == c).astype(BF16)
    kern = functools.partial(_qkv_kernel, n_norm_pieces=n_norm_cols // sub, sub=sub)
    return pl.pallas_call(
        kern,
        grid=(T // tm, N // tn),
        in_specs=[pl.BlockSpec((tm, D), lambda i, j: (i, 0)),
                  pl.BlockSpec((D, tn), lambda i, j: (0, j)),
                  pl.BlockSpec((1, tn), lambda i, j: (0, j)),
                  pl.BlockSpec((sub, sub), lambda i, j: (0, 0))],
        out_specs=pl.BlockSpec((tm, tn), lambda i, j: (i, j)),
        out_shape=jax.ShapeDtypeStruct((T, N), BF16),
        compiler_params=_params("arbitrary", "arbitrary"),
        name="swa_qkv_proj",
    )(h, w, gain_row, blk)


def _swa_kernel(q_ref, kp_ref, kc_ref, vp_ref, vc_ref, sink_ref, o_ref):
    blk = pl.program_id(1)
    BLK, hd = SW_BLOCK, SW_HEAD_DIM
    G = SW_HEADS // SW_KV_HEADS
    pairs = G // 2
    upairs = SWA_PAIRS_PER_DOT
    rows, band = upairs * BLK, 2 * BLK
    qi = lax.broadcasted_iota(jnp.int32, (rows, band), 0) & (BLK - 1)
    kr = lax.broadcasted_iota(jnp.int32, (rows, band), 1)
    rel = BLK + qi - kr
    first = jnp.where(blk > 0, 0, BLK)
    bias = jnp.where((rel >= 0) & (rel < BLK) & (kr >= first), 0.0, -jnp.inf)
    lane = lax.broadcasted_iota(jnp.int32, (1, LANES), 1)
    lo = (lane < hd).astype(BF16)
    hi = (lane >= hd).astype(BF16)
    lo_f = lane < hd
    ones_lo = jnp.broadcast_to(lo, (band, LANES))
    ones_hi = jnp.broadcast_to(hi, (band, LANES))
    units = [(kh, u) for kh in range(SW_KV_HEADS) for u in range(pairs // upairs)]

    scores = {}
    for kh in range(SW_KV_HEADS):
        cols = slice(kh * LANES, (kh + 1) * LANES)
        kd = jnp.concatenate([kp_ref[:, cols], kc_ref[:, cols]], axis=0)
        kz = jnp.concatenate([kd * lo, kd * hi], axis=0)
        for u in range(pairs // upairs):
            p0 = kh * pairs + u * upairs
            q = jnp.concatenate([q_ref[:, (p0 + p) * LANES:(p0 + p + 1) * LANES] for p in range(upairs)], axis=0)
            scores[kh, u] = lax.dot_general(q, kz, (((1,), (1,)), ((), ())), preferred_element_type=F32)

    vz = None
    for kh, u in units:
        if u == 0:
            cols = slice(kh * LANES, (kh + 1) * LANES)
            vd = jnp.concatenate([vp_ref[:, cols], vc_ref[:, cols]], axis=0)
            vz = jnp.concatenate([jnp.concatenate([vd * lo, ones_lo], axis=1),
                                  jnp.concatenate([vd * hi, ones_hi], axis=1)], axis=0)
        p0 = kh * pairs + u * upairs
        s = scores[kh, u]
        sink_b = sink_ref[kh, u * rows:(u + 1) * rows, :]
        es, mxs = [], []
        for half in range(2):
            sh = s[:, half * band:(half + 1) * band] + bias
            mx = jnp.maximum(jnp.max(sh, axis=1, keepdims=True), sink_b[:, half * hd:half * hd + 1])
            es.append(jnp.exp(sh - mx).astype(BF16))
            mxs.append(mx)
        o2 = _dot(jnp.concatenate(es, axis=1), vz)
        o = o2[:, :LANES] / (o2[:, LANES:] + jnp.exp(sink_b - jnp.where(lo_f, mxs[0], mxs[1])))
        for p in range(upairs):
            o_ref[:, (p0 + p) * LANES:(p0 + p + 1) * LANES] = o[p * BLK:(p + 1) * BLK].astype(BF16)


def _swa(qkv, sink_cols, batch, seq):
    T = qkv.shape[0]
    BLK = SW_BLOCK
    nb = seq // BLK
    qw = SW_HEADS * SW_HEAD_DIM
    kw = SW_KV_HEADS * LANES
    k_off = qw // kw
    v_off = k_off + 1

    def cur(b, j):
        return b * nb + j

    def prev(b, j):
        return b * nb + jnp.maximum(j - 1, 0)

    return pl.pallas_call(
        _swa_kernel,
        grid=(batch, nb),
        in_specs=[pl.BlockSpec((BLK, qw), lambda b, j: (cur(b, j), 0)),
                  pl.BlockSpec((BLK, kw), lambda b, j: (prev(b, j), k_off)),
                  pl.BlockSpec((BLK, kw), lambda b, j: (cur(b, j), k_off)),
                  pl.BlockSpec((BLK, kw), lambda b, j: (prev(b, j), v_off)),
                  pl.BlockSpec((BLK, kw), lambda b, j: (cur(b, j), v_off)),
                  pl.BlockSpec(sink_cols.shape, lambda b, j: (0, 0, 0))],
        out_specs=pl.BlockSpec((BLK, qw), lambda b, j: (cur(b, j), 0)),
        out_shape=jax.ShapeDtypeStruct((T, qw), BF16),
        compiler_params=_params("arbitrary", "arbitrary"),
        name="swa_attention",
    )(qkv, qkv, qkv, qkv, qkv, sink_cols)


def kernel(x, c, l0_w_mod, l0_b_mod, l0_w_in, l0_gate_bias, l0_h_norm, l0_w_out, l0_w_up, l0_conv_w, l0_conv_b,
           l0_w_down, l1_w_mod, l1_b_mod, l1_w_qkv, l1_q_norm, l1_k_norm, l1_sinks, l1_w_out, l1_w_up, l1_conv_w,
           l1_conv_b, l1_w_down):
    B, S, D = x.shape
    T = B * S
    H, dk, dv = ML_HEADS, ML_DQK, ML_DV
    x2d = x.reshape(T, D)

    mod0 = _modulation(c, l0_w_mod, l0_b_mod).reshape(B, 6, D)
    mod1 = _modulation(c, l1_w_mod, l1_b_mod).reshape(B, 6, D)

    n_main = 2 * H * dk + 2 * H * dv
    w_main = l0_w_in.astype(BF16)
    w_gate = jnp.pad(l0_w_in[:, n_main:], ((0, 0), (0, LANES - 2 * H))).astype(BF16)
    gate_bias = jnp.pad(l0_gate_bias, (0, LANES - 2 * H)).reshape(1, LANES)
    col_scale = jnp.concatenate([jnp.ones((H * dk,), F32), jnp.full((H * dk,), dk ** -0.5, F32),
                                 jnp.ones((2 * H * dv,), F32)]).reshape(1, n_main)
    hm = _first_norm(x2d, mod0, S)
    qkvo, gcol, grow = _inproj(hm, w_main, col_scale, w_gate, gate_bias)
    y = _mlstm(qkvo, gcol, grow, l0_h_norm.reshape(1, H * dv), B, S)
    x1, hf = _outproj(y, l0_w_out.astype(BF16), x2d, mod0, S)
    act = _ffn_up(hf, l0_w_up.astype(BF16), l0_conv_w, l0_conv_b, S)
    x2, hm1 = _ffn_down(act, l0_w_down.astype(BF16), x1, mod0, mod1, S)

    Hq, Hk, hd = SW_HEADS, SW_KV_HEADS, SW_HEAD_DIM
    scale = hd ** -0.5

    def dup_heads(w):
        w = w.reshape(D, Hk, 1, hd)
        return jnp.concatenate([w, w], axis=2).reshape(D, Hk * 2 * hd)

    w_qkv = jnp.concatenate([l1_w_qkv[:, :Hq * hd], dup_heads(l1_w_qkv[:, Hq * hd:(Hq + Hk) * hd]),
                             dup_heads(l1_w_qkv[:, (Hq + Hk) * hd:])], axis=1).astype(BF16)
    gain = jnp.concatenate([jnp.tile(l1_q_norm * scale, Hq), jnp.tile(l1_k_norm, 2 * Hk),
                            jnp.ones((2 * Hk * hd,), F32)]).reshape(1, -1)
    qkv = _qkv_proj(hm1, w_qkv, gain, n_norm_cols=(Hq + 2 * Hk) * hd)
    pairs = Hq // Hk // 2
    sink_cols = jnp.repeat(jnp.repeat(l1_sinks.reshape(Hk, pairs, 2), SW_BLOCK, axis=1), hd, axis=2)
    att = _swa(qkv, sink_cols, B, S)
    x3, hf1 = _outproj(att, l1_w_out.astype(BF16), x2, mod1, S)
    act1 = _ffn_up(hf1, l1_w_up.astype(BF16), l1_conv_w, l1_conv_b, S)
    (x4,) = _ffn_down(act1, l1_w_down.astype(BF16), x3, mod1, None, S)
    return x4.reshape(B, S, D)
```

```python
import functools
import math

import jax
import jax.numpy as jnp
from jax import lax
from jax.experimental import pallas as pl
from jax.experimental.pallas import tpu as pltpu

EPS = 1e-6
ML_HEADS = 8
ML_DQK = 128
ML_DV = 256
GATE_CAP = 15.0
SW_HEADS = 32
SW_KV_HEADS = 4
SW_HEAD_DIM = 64
SW_BLOCK = 128
CONV_WIDTH = 3
FFN_DOT_ROWS = 512
FFN_EPILOGUE_ROWS = 32

LANES = 128
SUBLANES = 8
VMEM_LIMIT = 56 * 1024 * 1024

F32 = jnp.float32
BF16 = jnp.bfloat16


def _params(*sem):
    return pltpu.CompilerParams(dimension_semantics=sem, vmem_limit_bytes=VMEM_LIMIT)


def _dot(a, b):
    return jnp.dot(a, b, preferred_element_type=F32)


def _rms_mod(x, scale_row, shift_row):
    ms = jnp.mean(x * x, axis=-1, keepdims=True)
    return (x * lax.rsqrt(ms + EPS)) * (1.0 + scale_row) + shift_row


def _mod_kernel(c_ref, w_ref, b_ref, o_ref):
    c = c_ref[...]
    sc = (c * jax.nn.sigmoid(c)).astype(BF16)
    o_ref[...] = _dot(sc, w_ref[...].astype(BF16)) + b_ref[...]


def _modulation(c, w_mod, b_mod, tn=1024):
    B, D = c.shape
    N = w_mod.shape[1]
    return pl.pallas_call(
        _mod_kernel,
        grid=(N // tn,),
        in_specs=[pl.BlockSpec((B, D), lambda j: (0, 0)),
                  pl.BlockSpec((D, tn), lambda j: (0, j)),
                  pl.BlockSpec((1, tn), lambda j: (0, j))],
        out_specs=pl.BlockSpec((B, tn), lambda j: (0, j)),
        out_shape=jax.ShapeDtypeStruct((B, N), F32),
        compiler_params=_params("arbitrary"),
        name="modulation",
    )(c, w_mod, b_mod.reshape(1, N))


def _norm_kernel(x_ref, mod_ref, o_ref):
    o_ref[...] = _rms_mod(x_ref[...], mod_ref[0, 1:2, :], mod_ref[0, 0:1, :]).astype(BF16)


def _first_norm(x2d, mod, seq, tm=512):
    T, D = x2d.shape
    per = seq // tm
    return pl.pallas_call(
        _norm_kernel,
        grid=(T // tm,),
        in_specs=[pl.BlockSpec((tm, D), lambda i: (i, 0)),
                  pl.BlockSpec((1, 6, D), lambda i: (i // per, 0, 0))],
        out_specs=pl.BlockSpec((tm, D), lambda i: (i, 0)),
        out_shape=jax.ShapeDtypeStruct((T, D), BF16),
        compiler_params=_params("arbitrary"),
        name="first_norm",
    )(x2d, mod)


def _inproj_kernel(h_ref, w_ref, cs_ref, wg_ref, gb_ref, o_ref, gcol_ref, grow_ref):
    j = pl.program_id(1)
    h = h_ref[...]
    o_ref[...] = (_dot(h, w_ref[...]) * cs_ref[...]).astype(BF16)

    @pl.when(j == 0)
    def _():
        g = _dot(h, wg_ref[...]) + gb_ref[...]
        g = GATE_CAP * jnp.tanh(g * (1.0 / GATE_CAP))
        log_f = jnp.minimum(g, 0.0) - jnp.log(1.0 + jnp.exp(-jnp.abs(g)))
        lane = lax.broadcasted_iota(jnp.int32, g.shape, 1)
        g = jnp.where(lane < ML_HEADS, g, log_f)
        gcol_ref[...] = g
        grow_ref[...] = jnp.transpose(g)[: 2 * ML_HEADS, :]


def _inproj(h, w_main, col_scale, w_gate, gate_bias, tm=2048, tn=1024):
    T, D = h.shape
    N = col_scale.shape[1]
    return pl.pallas_call(
        _inproj_kernel,
        grid=(T // tm, N // tn),
        in_specs=[pl.BlockSpec((tm, D), lambda i, j: (i, 0)),
                  pl.BlockSpec((D, tn), lambda i, j: (0, j)),
                  pl.BlockSpec((1, tn), lambda i, j: (0, j)),
                  pl.BlockSpec((D, LANES), lambda i, j: (0, 0)),
                  pl.BlockSpec((1, LANES), lambda i, j: (0, 0))],
        out_specs=[pl.BlockSpec((tm, tn), lambda i, j: (i, j)),
                   pl.BlockSpec((tm, LANES), lambda i, j: (i, 0)),
                   pl.BlockSpec((2 * ML_HEADS, tm), lambda i, j: (0, i))],
        out_shape=[jax.ShapeDtypeStruct((T, N), BF16),
                   jax.ShapeDtypeStruct((T, LANES), F32),
                   jax.ShapeDtypeStruct((2 * ML_HEADS, T), F32)],
        compiler_params=_params("arbitrary", "arbitrary"),
        name="mlstm_inproj",
    )(h, w_main, col_scale, w_gate, gate_bias)


def _split3(x):
    hi = x.astype(BF16)
    r1 = x - hi.astype(F32)
    mid = r1.astype(BF16)
    lo = (r1 - mid.astype(F32)).astype(BF16)
    return hi, mid, lo


def _mlstm_kernel(q_ref, k_ref, v_ref, o_ref, gcol_ref, grow_ref, hn_ref, y_ref,
                  dexp_ref, wint_ref, floor_ref, wcol_ref, decay_ref, cn_ref,
                  *, heads_per_step, seq):
    L = LANES
    hg = heads_per_step
    nc = seq // L
    dk, dv = ML_DQK, ML_DV
    h0 = pl.program_id(1) * hg
    row = lax.broadcasted_iota(jnp.int32, (L, L), 0)
    col = lax.broadcasted_iota(jnp.int32, (L, L), 1)
    causal = col <= row
    tril = jnp.where(causal, 1.0, 0.0).astype(BF16)
    triu = jnp.where(row <= col, 1.0, 0.0).astype(BF16)

    sel_r = lax.broadcasted_iota(jnp.int32, (L, hg * L), 0)
    sel_head = h0 + lax.broadcasted_iota(jnp.int32, (L, hg * L), 1) // L
    sel_i = jnp.where(sel_r == sel_head, 1.0, 0.0).astype(BF16)
    sel_f = jnp.where(sel_r == sel_head + ML_HEADS, 1.0, 0.0).astype(BF16)
    gate_row = lax.broadcasted_iota(jnp.int32, (2 * ML_HEADS, 1), 0)
    m_run = [jnp.zeros((1, L), F32) for _ in range(hg)]
    spread = []
    for c in range(nc):
        rows = slice(c * L, (c + 1) * L)
        terms = _split3(gcol_ref[rows, :])
        ib_all = sum(_dot(t, sel_i) for t in terms)
        f_terms = [_dot(t, sel_f).astype(BF16) for t in terms]
        grow = grow_ref[:, rows]
        cgrow = sum(_dot(t, triu) for t in _split3(grow))
        spread.append((ib_all, f_terms, grow, cgrow))
    for c in range(nc):
        ib_all, f_terms, grow, cgrow = spread[c]
        bb_all = sum(_dot(tril, t) for t in f_terms)
        for hh in range(hg):
            head = h0 + hh
            m_c = m_run[hh]
            ib, bb = ib_all[:, hh * L:(hh + 1) * L], bb_all[:, hh * L:(hh + 1) * L]
            a_row = jnp.sum(jnp.where(gate_row == head, grow, 0.0)
                            - jnp.where(gate_row == head + ML_HEADS, cgrow, 0.0),
                            axis=0, keepdims=True)
            ca = jnp.max(jnp.where(causal, a_row, -jnp.inf), axis=1, keepdims=True)
            mm = jnp.maximum(m_c, jnp.broadcast_to(ca, (L, L)))
            total = bb[L - 1:L, :]
            m_next = total + mm[L - 1:L, :]
            dexp_ref[hh, c] = jnp.exp(jnp.where(causal, a_row - mm, -jnp.inf))
            wint_ref[hh, c] = jnp.exp(m_c - mm)
            floor_ref[hh, c] = jnp.exp(-(bb + mm))
            wcol_ref[hh, c] = jnp.exp(total + (ib - bb) - m_next)
            decay_ref[hh, c] = jnp.broadcast_to(jnp.exp(total + m_c - m_next), (SUBLANES, L))
            m_run[hh] = m_next

    cn_ref[...] = jnp.zeros_like(cn_ref)
    ones = jnp.ones((L, L), BF16)

    def chunk_step(ci, carry):
        r0 = pl.multiple_of(ci * L, L)
        early = []
        for hh in range(hg):
            q = q_ref[pl.ds(r0, L), hh * dk:(hh + 1) * dk]
            k = k_ref[pl.ds(r0, L), hh * dk:(hh + 1) * dk]
            v = v_ref[pl.ds(r0, L), hh * dv:(hh + 1) * dv]
            v_ext = jnp.concatenate([v, ones], axis=1)
            cn = cn_ref[hh]
            k_w = (k.astype(F32) * wcol_ref[hh, ci]).astype(BF16)
            kv = lax.dot_general(k_w, v_ext, (((0,), (0,)), ((), ())), preferred_element_type=F32)
            s = lax.dot_general(q, k, (((1,), (1,)), ((), ())), preferred_element_type=F32)
            inter = _dot(q, cn.astype(BF16))
            dec = decay_ref[hh, ci][0:1, :]
            cn_ref[hh] = jnp.concatenate([dec] * (cn.shape[1] // L), axis=1) * cn + kv
            early.append((s, inter, v_ext))
        for hh in range(hg):
            s, inter, v_ext = early[hh]
            p = (dexp_ref[hh, ci] * s).astype(BF16)
            intra = _dot(p, v_ext)
            wint = wint_ref[hh, ci]
            num = jnp.concatenate([wint, wint], axis=1) * inter[:, :dv] + intra[:, :dv]
            den = wint * inter[:, dv:] + intra[:, dv:]
            rdn = 1.0 / jnp.maximum(jnp.abs(den), floor_ref[hh, ci])
            hc = num * jnp.concatenate([rdn, rdn], axis=1)
            hs = hc * lax.rsqrt(jnp.mean(hc * hc, axis=1, keepdims=True) + EPS) * hn_ref[:, hh * dv:(hh + 1) * dv]
            og = o_ref[pl.ds(r0, L), hh * dv:(hh + 1) * dv].astype(F32)
            y_ref[pl.ds(r0, L), hh * dv:(hh + 1) * dv] = (jax.nn.sigmoid(og) * hs).astype(BF16)
        return carry

    lax.fori_loop(0, nc, chunk_step, 0, unroll=4)


def _mlstm(qkvo, gcol, grow, h_norm, batch, seq, heads_per_step=2):
    T = qkvo.shape[0]
    H, dk, dv = ML_HEADS, ML_DQK, ML_DV
    assert dk == LANES and seq % LANES == 0
    hg = heads_per_step
    nc = seq // LANES
    qb, vb = hg * dk, hg * dv
    k_off = (H * dk) // qb
    v_off = (2 * H * dk) // vb
    o_off = (2 * H * dk + H * dv) // vb
    kern = functools.partial(_mlstm_kernel, heads_per_step=hg, seq=seq)
    factor = pltpu.VMEM((hg, nc, LANES, LANES), F32)
    return pl.pallas_call(
        kern,
        grid=(batch, H // hg),
        in_specs=[pl.BlockSpec((seq, qb), lambda b, g: (b, g)),
                  pl.BlockSpec((seq, qb), lambda b, g: (b, k_off + g)),
                  pl.BlockSpec((seq, vb), lambda b, g: (b, v_off + g)),
                  pl.BlockSpec((seq, vb), lambda b, g: (b, o_off + g)),
                  pl.BlockSpec((seq, LANES), lambda b, g: (b, 0)),
                  pl.BlockSpec((2 * H, seq), lambda b, g: (0, b)),
                  pl.BlockSpec((1, vb), lambda b, g: (0, g))],
        out_specs=pl.BlockSpec((seq, vb), lambda b, g: (b, g)),
        out_shape=jax.ShapeDtypeStruct((T, H * dv), BF16),
        scratch_shapes=[factor, factor, factor, factor,
                        pltpu.VMEM((hg, nc, SUBLANES, LANES), F32),
                        pltpu.VMEM((hg, dk, dv + LANES), F32)],
        compiler_params=_params("arbitrary", "arbitrary"),
        name="mlstm_chunks",
    )(qkvo, qkvo, qkvo, qkvo, gcol, grow, h_norm)


def _outproj_kernel(y_ref, w_ref, x_ref, mod_ref, xo_ref, ho_ref):
    acc = _dot(y_ref[...], w_ref[...])
    x_new = x_ref[...] + mod_ref[0, 2:3, :] * acc
    xo_ref[...] = x_new
    ho_ref[...] = _rms_mod(x_new, mod_ref[0, 4:5, :], mod_ref[0, 3:4, :]).astype(BF16)


def _outproj(y, w, x2d, mod, seq, tm=512):
    T, K = y.shape
    D = w.shape[1]
    per = seq // tm
    return pl.pallas_call(
        _outproj_kernel,
        grid=(T // tm,),
        in_specs=[pl.BlockSpec((tm, K), lambda i: (i, 0)),
                  pl.BlockSpec((K, D), lambda i: (0, 0)),
                  pl.BlockSpec((tm, D), lambda i: (i, 0)),
                  pl.BlockSpec((1, 6, D), lambda i: (i // per, 0, 0))],
        out_specs=[pl.BlockSpec((tm, D), lambda i: (i, 0)),
                   pl.BlockSpec((tm, D), lambda i: (i, 0))],
        out_shape=[jax.ShapeDtypeStruct((T, D), F32),
                   jax.ShapeDtypeStruct((T, D), BF16)],
        compiler_params=_params("arbitrary"),
        name="outproj_residual_norm",
    )(y, w, x2d, mod)


def _ffn_up_kernel(h_ref, wg_ref, wv_ref, cwg_ref, cwv_ref, cbg_ref, cbv_ref, o_ref, ug_ref, uv_ref, *, seq, sub):
    rb, rc = FFN_DOT_ROWS, FFN_EPILOGUE_ROWS
    units = [(s, b0) for s in range(o_ref.shape[1] // sub) for b0 in range(0, seq, rb)]

    def matmuls(n):
        s, b0 = units[n]
        cols = slice(s * sub, (s + 1) * sub)
        h = h_ref[b0:b0 + rb, :]
        for u_ref, w_ref in ((ug_ref, wg_ref), (uv_ref, wv_ref)):
            u_ref[n % 2, :SUBLANES, :] = (u_ref[(n - 1) % 2, rb:, :] if b0 else jnp.zeros((SUBLANES, sub), F32))
            u_ref[n % 2, SUBLANES:, :] = _dot(h, w_ref[:, cols].astype(BF16))

    def conv(u_ref, n, r0, cw, cb):
        out = cb + cw[2:3, :] * u_ref[n % 2, SUBLANES + r0:SUBLANES + r0 + rc, :]
        out = out + cw[1:2, :] * u_ref[n % 2, SUBLANES - 1 + r0:SUBLANES - 1 + r0 + rc, :]
        out = out + cw[0:1, :] * u_ref[n % 2, SUBLANES - 2 + r0:SUBLANES - 2 + r0 + rc, :]
        return out

    def epilogue(n):
        s, b0 = units[n]
        cols = slice(s * sub, (s + 1) * sub)
        cwg, cwv, cbg, cbv = cwg_ref[:, cols], cwv_ref[:, cols], cbg_ref[:, cols], cbv_ref[:, cols]
        for r0 in range(0, rb, rc):
            g = conv(ug_ref, n, r0, cwg, cbg)
            v = conv(uv_ref, n, r0, cwv, cbv)
            o_ref[b0 + r0:b0 + r0 + rc, cols] = (g * jax.nn.sigmoid(g) * v).astype(BF16)

    for n in range(len(units)):
        matmuls(n)
        if n:
            epilogue(n - 1)
    epilogue(len(units) - 1)


def _ffn_up(h, w_up, conv_w, conv_b, seq, tn=512, sub=256):
    T, D = h.shape
    F = w_up.shape[1] // 2
    nf = F // tn
    kern = functools.partial(_ffn_up_kernel, seq=seq, sub=sub)
    cb = conv_b.reshape(1, 2 * F)
    return pl.pallas_call(
        kern,
        grid=(T // seq, nf),
        in_specs=[pl.BlockSpec((seq, D), lambda i, j: (i, 0)),
                  pl.BlockSpec((D, tn), lambda i, j: (0, j)),
                  pl.BlockSpec((D, tn), lambda i, j: (0, nf + j)),
                  pl.BlockSpec((CONV_WIDTH, tn), lambda i, j: (0, j)),
                  pl.BlockSpec((CONV_WIDTH, tn), lambda i, j: (0, nf + j)),
                  pl.BlockSpec((1, tn), lambda i, j: (0, j)),
                  pl.BlockSpec((1, tn), lambda i, j: (0, nf + j))],
        out_specs=pl.BlockSpec((seq, tn), lambda i, j: (i, j)),
        out_shape=jax.ShapeDtypeStruct((T, F), BF16),
        scratch_shapes=[pltpu.VMEM((2, SUBLANES + FFN_DOT_ROWS, sub), F32),
                        pltpu.VMEM((2, SUBLANES + FFN_DOT_ROWS, sub), F32)],
        compiler_params=_params("arbitrary", "arbitrary"),
        name="ffn_up_conv_act",
    )(h, w_up, w_up, conv_w, conv_w, cb, cb)


def _ffn_down_kernel(a_ref, w_ref, x_ref, mod_ref, *rest, emit_norm):
    x_new = x_ref[...] + mod_ref[0, 5:6, :] * _dot(a_ref[...], w_ref[...])
    if emit_norm:
        nmod_ref, xo_ref, ho_ref = rest
        ho_ref[...] = _rms_mod(x_new, nmod_ref[0, 1:2, :], nmod_ref[0, 0:1, :]).astype(BF16)
    else:
        (xo_ref,) = rest
    xo_ref[...] = x_new


def _ffn_down(a, w, x2d, mod, next_mod, seq, tm=256):
    T, K = a.shape
    D = w.shape[1]
    per = seq // tm
    emit_norm = next_mod is not None
    mod_spec = pl.BlockSpec((1, 6, D), lambda i: (i // per, 0, 0))
    row_spec = pl.BlockSpec((tm, D), lambda i: (i, 0))
    in_specs = [pl.BlockSpec((tm, K), lambda i: (i, 0)),
                pl.BlockSpec((K, D), lambda i: (0, 0), pipeline_mode=pl.Buffered(1)),
                row_spec, mod_spec]
    args = [a, w, x2d, mod]
    out_specs = [row_spec]
    out_shape = [jax.ShapeDtypeStruct((T, D), F32)]
    if emit_norm:
        in_specs.append(mod_spec)
        args.append(next_mod)
        out_specs.append(row_spec)
        out_shape.append(jax.ShapeDtypeStruct((T, D), BF16))
    return pl.pallas_call(
        functools.partial(_ffn_down_kernel, emit_norm=emit_norm),
        grid=(T // tm,),
        in_specs=in_specs,
        out_specs=out_specs,
        out_shape=out_shape,
        compiler_params=_params("arbitrary"),
        name="ffn_down_residual_norm",
    )(*args)


def _qkv_kernel(h_ref, w_ref, gn_ref, blk_ref, o_ref, *, n_norm_pieces, sub):
    j = pl.program_id(1)
    h = h_ref[...]
    tn = o_ref.shape[1]
    pieces = [slice(s, s + sub) for s in range(0, tn, sub)]

    def tile(n_norm_pieces):
        accs = [_dot(h, w_ref[:, cols]) for cols in pieces]
        for p, cols in enumerate(pieces):
            acc = accs[p]
            if p < n_norm_pieces:
                ss = _dot((acc * acc).astype(BF16), blk_ref[...])
                acc = acc * lax.rsqrt(ss * (1.0 / SW_HEAD_DIM) + EPS) * gn_ref[:, cols]
            o_ref[:, cols] = acc.astype(BF16)

    full, rest = divmod(n_norm_pieces, len(pieces))

    @pl.when(j < full)
    def _():
        tile(len(pieces))

    @pl.when(j == full)
    def _():
        tile(rest)

    @pl.when(j > full)
    def _():
        tile(0)


def _qkv_proj(h, w, gain_row, n_norm_cols, tm=2048, tn=1024, sub=256):
    T, D = h.shape
    N = w.shape[1]
    r = lax.broadcasted_iota(jnp.int32, (sub, sub), 0) // SW_HEAD_DIM
    c = lax.broadcasted_iota(jnp.int32, (sub, sub), 1) // SW_HEAD_DIM
    blk = (r == c).astype(BF16)
    kern = functools.partial(_qkv_kernel, n_norm_pieces=n_norm_cols // sub, sub=sub)
    return pl.pallas_call(
        kern,
        grid=(T // tm, N // tn),
        in_specs=[pl.BlockSpec((tm, D), lambda i, j: (i, 0)),
                  pl.BlockSpec((D, tn), lambda i, j: (0, j)),
                  pl.BlockSpec((1, tn), lambda i, j: (0, j)),
                  pl.BlockSpec((sub, sub), lambda i, j: (0, 0))],
        out_specs=pl.BlockSpec((tm, tn), lambda i, j: (i, j)),
        out_shape=jax.ShapeDtypeStruct((T, N), BF16),
        compiler_params=_params("arbitrary", "arbitrary"),
        name="swa_qkv_proj",
    )(h, w, gain_row, blk)


def _swa_kernel(q_ref, kp_ref, kc_ref, vp_ref, vc_ref, sink_ref, o_ref):
    step = pl.program_id(1)
    BLK, hd = SW_BLOCK, SW_HEAD_DIM
    G = SW_HEADS // SW_KV_HEADS
    pairs = G // 2
    band = 2 * BLK
    nq = q_ref.shape[0] // BLK
    qi = lax.broadcasted_iota(jnp.int32, (BLK, band), 0)
    kr = lax.broadcasted_iota(jnp.int32, (BLK, band), 1)
    rel = BLK + qi - kr
    in_window = (rel >= 0) & (rel < BLK)
    bias_any = jnp.where(in_window, 0.0, -jnp.inf)
    bias_first = jnp.where(in_window & (kr >= jnp.where(step > 0, 0, BLK)), 0.0, -jnp.inf)
    lane = lax.broadcasted_iota(jnp.int32, (1, LANES), 1)
    lo = (lane < hd).astype(BF16)
    hi = (lane >= hd).astype(BF16)
    lo_f = lane < hd
    ones_lo = jnp.broadcast_to(lo, (band, LANES))
    ones_hi = jnp.broadcast_to(hi, (band, LANES))

    def band_rows(prev_ref, cur_ref, qb, cols):
        before = prev_ref[:, cols] if qb == 0 else cur_ref[(qb - 1) * BLK:qb * BLK, cols]
        return jnp.concatenate([before, cur_ref[qb * BLK:(qb + 1) * BLK, cols]], axis=0)

    units = [(qb, kh, p) for qb in range(nq) for kh in range(SW_KV_HEADS) for p in range(pairs)]

    scores = {}
    for qb in range(nq):
        for kh in range(SW_KV_HEADS):
            kd = band_rows(kp_ref, kc_ref, qb, slice(kh * LANES, (kh + 1) * LANES))
            kz = jnp.concatenate([kd * lo, kd * hi], axis=0)
            for p in range(pairs):
                q = q_ref[qb * BLK:(qb + 1) * BLK, (kh * pairs + p) * LANES:(kh * pairs + p + 1) * LANES]
                scores[qb, kh, p] = lax.dot_general(q, kz, (((1,), (1,)), ((), ())), preferred_element_type=F32)

    vz = None
    for qb, kh, p in units:
        if p == 0:
            vd = band_rows(vp_ref, vc_ref, qb, slice(kh * LANES, (kh + 1) * LANES))
            vz = jnp.concatenate([jnp.concatenate([vd * lo, ones_lo], axis=1),
                                  jnp.concatenate([vd * hi, ones_hi], axis=1)], axis=0)
        bias = bias_first if qb == 0 else bias_any
        s = scores[qb, kh, p]
        sink_b = sink_ref[kh, p * BLK:(p + 1) * BLK, :]
        es, mxs = [], []
        for half in range(2):
            sh = s[:, half * band:(half + 1) * band] + bias
            mx = jnp.maximum(jnp.max(sh, axis=1, keepdims=True), sink_b[:, half * hd:half * hd + 1])
            es.append(jnp.exp(sh - mx).astype(BF16))
            mxs.append(mx)
        o2 = _dot(jnp.concatenate(es, axis=1), vz)
        o = o2[:, :LANES] / (o2[:, LANES:] + jnp.exp(sink_b - jnp.where(lo_f, mxs[0], mxs[1])))
        o_ref[qb * BLK:(qb + 1) * BLK, (kh * pairs + p) * LANES:(kh * pairs + p + 1) * LANES] = o.astype(BF16)


def _swa(qkv, sink_cols, batch, seq, q_blocks=4):
    T = qkv.shape[0]
    BLK = SW_BLOCK
    rows = q_blocks * BLK
    ns = seq // rows
    qw = SW_HEADS * SW_HEAD_DIM
    kw = SW_KV_HEADS * LANES
    k_off = qw // kw
    v_off = k_off + 1

    def cur(b, j):
        return b * ns + j

    def prev(b, j):
        return (b * ns + j) * q_blocks - jnp.minimum(j, 1)

    return pl.pallas_call(
        _swa_kernel,
        grid=(batch, ns),
        in_specs=[pl.BlockSpec((rows, qw), lambda b, j: (cur(b, j), 0)),
                  pl.BlockSpec((BLK, kw), lambda b, j: (prev(b, j), k_off)),
                  pl.BlockSpec((rows, kw), lambda b, j: (cur(b, j), k_off)),
                  pl.BlockSpec((BLK, kw), lambda b, j: (prev(b, j), v_off)),
                  pl.BlockSpec((rows, kw), lambda b, j: (cur(b, j), v_off)),
                  pl.BlockSpec(sink_cols.shape, lambda b, j: (0, 0, 0))],
        out_specs=pl.BlockSpec((rows, qw), lambda b, j: (cur(b, j), 0)),
        out_shape=jax.ShapeDtypeStruct((T, qw), BF16),
        compiler_params=_params("arbitrary", "arbitrary"),
        name="swa_attention",
    )(qkv, qkv, qkv, qkv, qkv, sink_cols)


def kernel(x, c, l0_w_mod, l0_b_mod, l0_w_in, l0_gate_bias, l0_h_norm, l0_w_out, l0_w_up, l0_conv_w, l0_conv_b,
           l0_w_down, l1_w_mod, l1_b_mod, l1_w_qkv, l1_q_norm, l1_k_norm, l1_sinks, l1_w_out, l1_w_up, l1_conv_w,
           l1_conv_b, l1_w_down):
    B, S, D = x.shape
    T = B * S
    H, dk, dv = ML_HEADS, ML_DQK, ML_DV
    x2d = x.reshape(T, D)

    mod0 = _modulation(c, l0_w_mod, l0_b_mod).reshape(B, 6, D)
    mod1 = _modulation(c, l1_w_mod, l1_b_mod).reshape(B, 6, D)

    n_main = 2 * H * dk + 2 * H * dv
    w_main = l0_w_in.astype(BF16)
    w_gate = jnp.pad(l0_w_in[:, n_main:], ((0, 0), (0, LANES - 2 * H))).astype(BF16)
    gate_bias = jnp.pad(l0_gate_bias, (0, LANES - 2 * H)).reshape(1, LANES)
    col_scale = jnp.concatenate([jnp.ones((H * dk,), F32), jnp.full((H * dk,), dk ** -0.5, F32),
                                 jnp.ones((2 * H * dv,), F32)]).reshape(1, n_main)
    hm = _first_norm(x2d, mod0, S)
    qkvo, gcol, grow = _inproj(hm, w_main, col_scale, w_gate, gate_bias)
    y = _mlstm(qkvo, gcol, grow, l0_h_norm.reshape(1, H * dv), B, S)
    x1, hf = _outproj(y, l0_w_out.astype(BF16), x2d, mod0, S)
    act = _ffn_up(hf, l0_w_up, l0_conv_w, l0_conv_b, S)
    x2, hm1 = _ffn_down(act, l0_w_down.astype(BF16), x1, mod0, mod1, S)

    Hq, Hk, hd = SW_HEADS, SW_KV_HEADS, SW_HEAD_DIM
    scale = hd ** -0.5

    def dup_heads(w):
        w = w.reshape(D, Hk, 1, hd)
        return jnp.concatenate([w, w], axis=2).reshape(D, Hk * 2 * hd)

    w_qkv = jnp.concatenate([l1_w_qkv[:, :Hq * hd], dup_heads(l1_w_qkv[:, Hq * hd:(Hq + Hk) * hd]),
                             dup_heads(l1_w_qkv[:, (Hq + Hk) * hd:])], axis=1).astype(BF16)
    gain = jnp.concatenate([jnp.tile(l1_q_norm * scale, Hq), jnp.tile(l1_k_norm, 2 * Hk),
                            jnp.ones((2 * Hk * hd,), F32)]).reshape(1, -1)
    qkv = _qkv_proj(hm1, w_qkv, gain, n_norm_cols=(Hq + 2 * Hk) * hd)
    pairs = Hq // Hk // 2
    sink_cols = jnp.repeat(jnp.repeat(l1_sinks.reshape(Hk, pairs, 2), SW_BLOCK, axis=1), hd, axis=2)
    att = _swa(qkv, sink_cols, B, S)
    x3, hf1 = _outproj(att, l1_w_out.astype(BF16), x2, mod1, S)
    act1 = _ffn_up(hf1, l1_w_up, l1_conv_w, l1_conv_b, S)
    (x4,) = _ffn_down(act1, l1_w_down.astype(BF16), x3, mod1, None, S)
    return x4.reshape(B, S, D)
```

```python
import functools
import math

import jax
import jax.numpy as jnp
from jax import lax
from jax.experimental import pallas as pl
from jax.experimental.pallas import tpu as pltpu

EPS = 1e-6
ML_HEADS = 8
ML_DQK = 128
ML_DV = 256
GATE_CAP = 15.0
SW_HEADS = 32
SW_KV_HEADS = 4
SW_HEAD_DIM = 64
SW_BLOCK = 128
CONV_WIDTH = 3
MLSTM_CHUNK_UNROLL = 4
FFN_DOT_ROWS = 512
FFN_EPILOGUE_ROWS = 32

LANES = 128
SUBLANES = 8
VMEM_LIMIT = 56 * 1024 * 1024

F32 = jnp.float32
BF16 = jnp.bfloat16


def _params(*sem):
    return pltpu.CompilerParams(dimension_semantics=sem, vmem_limit_bytes=VMEM_LIMIT)


def _dot(a, b):
    return jnp.dot(a, b, preferred_element_type=F32)


def _rms_mod(x, scale_row, shift_row):
    ms = jnp.mean(x * x, axis=-1, keepdims=True)
    return (x * lax.rsqrt(ms + EPS)) * (1.0 + scale_row) + shift_row


def _mod_kernel(c_ref, w_ref, b_ref, o_ref):
    c = c_ref[...]
    sc = (c * jax.nn.sigmoid(c)).astype(BF16)
    o_ref[...] = _dot(sc, w_ref[...].astype(BF16)) + b_ref[...]


def _modulation(c, w_mod, b_mod, tn=1024):
    B, D = c.shape
    N = w_mod.shape[1]
    return pl.pallas_call(
        _mod_kernel,
        grid=(N // tn,),
        in_specs=[pl.BlockSpec((B, D), lambda j: (0, 0)),
                  pl.BlockSpec((D, tn), lambda j: (0, j)),
                  pl.BlockSpec((1, tn), lambda j: (0, j))],
        out_specs=pl.BlockSpec((B, tn), lambda j: (0, j)),
        out_shape=jax.ShapeDtypeStruct((B, N), F32),
        compiler_params=_params("arbitrary"),
        name="modulation",
    )(c, w_mod, b_mod.reshape(1, N))


def _norm_kernel(x_ref, mod_ref, o_ref):
    o_ref[...] = _rms_mod(x_ref[...], mod_ref[0, 1:2, :], mod_ref[0, 0:1, :]).astype(BF16)


def _first_norm(x2d, mod, seq, tm=1024):
    T, D = x2d.shape
    per = seq // tm
    return pl.pallas_call(
        _norm_kernel,
        grid=(T // tm,),
        in_specs=[pl.BlockSpec((tm, D), lambda i: (i, 0)),
                  pl.BlockSpec((1, 6, D), lambda i: (i // per, 0, 0))],
        out_specs=pl.BlockSpec((tm, D), lambda i: (i, 0)),
        out_shape=jax.ShapeDtypeStruct((T, D), BF16),
        compiler_params=_params("arbitrary"),
        name="first_norm",
    )(x2d, mod)


def _inproj_kernel(h_ref, w_ref, cs_ref, wg_ref, gb_ref, o_ref, gcol_ref, grow_ref):
    j = pl.program_id(1)
    h = h_ref[...]
    o_ref[...] = (_dot(h, w_ref[...]) * cs_ref[...]).astype(BF16)

    @pl.when(j == 0)
    def _():
        g = _dot(h, wg_ref[...]) + gb_ref[...]
        g = GATE_CAP * jnp.tanh(g * (1.0 / GATE_CAP))
        log_f = jnp.minimum(g, 0.0) - jnp.log(1.0 + jnp.exp(-jnp.abs(g)))
        lane = lax.broadcasted_iota(jnp.int32, g.shape, 1)
        g = jnp.where(lane < ML_HEADS, g, log_f)
        gcol_ref[...] = g
        grow_ref[...] = jnp.transpose(g)[: 2 * ML_HEADS, :]


def _inproj(h, w_main, col_scale, w_gate, gate_bias, tm=2048, tn=1024):
    T, D = h.shape
    N = col_scale.shape[1]
    return pl.pallas_call(
        _inproj_kernel,
        grid=(T // tm, N // tn),
        in_specs=[pl.BlockSpec((tm, D), lambda i, j: (i, 0)),
                  pl.BlockSpec((D, tn), lambda i, j: (0, j)),
                  pl.BlockSpec((1, tn), lambda i, j: (0, j)),
                  pl.BlockSpec((D, LANES), lambda i, j: (0, 0)),
                  pl.BlockSpec((1, LANES), lambda i, j: (0, 0))],
        out_specs=[pl.BlockSpec((tm, tn), lambda i, j: (i, j)),
                   pl.BlockSpec((tm, LANES), lambda i, j: (i, 0)),
                   pl.BlockSpec((2 * ML_HEADS, tm), lambda i, j: (0, i))],
        out_shape=[jax.ShapeDtypeStruct((T, N), BF16),
                   jax.ShapeDtypeStruct((T, LANES), F32),
                   jax.ShapeDtypeStruct((2 * ML_HEADS, T), F32)],
        compiler_params=_params("arbitrary", "arbitrary"),
        name="mlstm_inproj",
    )(h, w_main, col_scale, w_gate, gate_bias)


def _split_bf16(x):
    hi = x.astype(BF16)
    lo = (x - hi.astype(F32)).astype(BF16)
    return hi, lo


def _mlstm_kernel(q_ref, k_ref, v_ref, o_ref, gcol_ref, grow_ref, hn_ref, y_ref,
                  dexp_ref, wint_ref, floor_ref, wcol_ref, decay_ref, cn_ref,
                  *, heads_per_step, seq):
    L = LANES
    hg = heads_per_step
    nc = seq // L
    dk, dv = ML_DQK, ML_DV
    h0 = pl.program_id(1) * hg
    row = lax.broadcasted_iota(jnp.int32, (L, L), 0)
    col = lax.broadcasted_iota(jnp.int32, (L, L), 1)
    causal = col <= row
    tril = jnp.where(causal, 1.0, 0.0).astype(BF16)
    triu = jnp.where(row <= col, 1.0, 0.0).astype(BF16)

    sel_r = lax.broadcasted_iota(jnp.int32, (L, hg * L), 0)
    sel_head = h0 + lax.broadcasted_iota(jnp.int32, (L, hg * L), 1) // L
    sel_i = jnp.where(sel_r == sel_head, 1.0, 0.0).astype(BF16)
    sel_f = jnp.where(sel_r == sel_head + ML_HEADS, 1.0, 0.0).astype(BF16)
    gate_row = lax.broadcasted_iota(jnp.int32, (2 * ML_HEADS, 1), 0)
    m_run = [jnp.zeros((1, L), F32) for _ in range(hg)]
    spread = []
    for c in range(nc):
        rows = slice(c * L, (c + 1) * L)
        terms = _split_bf16(gcol_ref[rows, :])
        ib_all = sum(_dot(t, sel_i) for t in terms)
        f_terms = [_dot(t, sel_f).astype(BF16) for t in terms]
        grow = grow_ref[:, rows]
        cgrow = sum(_dot(t, triu) for t in _split_bf16(grow))
        spread.append((ib_all, f_terms, grow, cgrow))
    for c in range(nc):
        ib_all, f_terms, grow, cgrow = spread[c]
        bb_all = sum(_dot(tril, t) for t in f_terms)
        for hh in range(hg):
            head = h0 + hh
            m_c = m_run[hh]
            ib, bb = ib_all[:, hh * L:(hh + 1) * L], bb_all[:, hh * L:(hh + 1) * L]
            a_row = jnp.sum(jnp.where(gate_row == head, grow, 0.0)
                            - jnp.where(gate_row == head + ML_HEADS, cgrow, 0.0),
                            axis=0, keepdims=True)
            ca = jnp.max(jnp.where(causal, a_row, -jnp.inf), axis=1, keepdims=True)
            mm = jnp.maximum(m_c, jnp.broadcast_to(ca, (L, L)))
            total = bb[L - 1:L, :]
            m_next = total + mm[L - 1:L, :]
            dexp_ref[hh, c] = jnp.exp(jnp.where(causal, a_row - mm, -jnp.inf))
            wint_ref[hh, c] = jnp.exp(m_c - mm)
            floor_ref[hh, c] = jnp.exp(-(bb + mm))
            wcol_ref[hh, c] = jnp.exp(total + (ib - bb) - m_next)
            decay_ref[hh, c] = jnp.broadcast_to(jnp.exp(total + m_c - m_next), (SUBLANES, L))
            m_run[hh] = m_next

    cn_ref[...] = jnp.zeros_like(cn_ref)
    ones = jnp.ones((L, L), BF16)

    def chunk_step(ci, carry):
        r0 = pl.multiple_of(ci * L, L)
        early = []
        for hh in range(hg):
            q = q_ref[pl.ds(r0, L), hh * dk:(hh + 1) * dk]
            k = k_ref[pl.ds(r0, L), hh * dk:(hh + 1) * dk]
            v = v_ref[pl.ds(r0, L), hh * dv:(hh + 1) * dv]
            v_ext = jnp.concatenate([v, ones], axis=1)
            cn = cn_ref[hh]
            k_w = (k.astype(F32) * wcol_ref[hh, ci]).astype(BF16)
            kv = lax.dot_general(k_w, v_ext, (((0,), (0,)), ((), ())), preferred_element_type=F32)
            s = lax.dot_general(q, k, (((1,), (1,)), ((), ())), preferred_element_type=F32)
            inter = _dot(q, cn.astype(BF16))
            dec = decay_ref[hh, ci][0:1, :]
            cn_ref[hh] = jnp.concatenate([dec] * (cn.shape[1] // L), axis=1) * cn + kv
            early.append((s, inter, v_ext))
        for hh in range(hg):
            s, inter, v_ext = early[hh]
            p = (dexp_ref[hh, ci] * s).astype(BF16)
            intra = _dot(p, v_ext)
            wint = wint_ref[hh, ci]
            num = jnp.concatenate([wint, wint], axis=1) * inter[:, :dv] + intra[:, :dv]
            den = wint * inter[:, dv:] + intra[:, dv:]
            rdn = 1.0 / jnp.maximum(jnp.abs(den), floor_ref[hh, ci])
            hc = num * jnp.concatenate([rdn, rdn], axis=1)
            hs = hc * lax.rsqrt(jnp.mean(hc * hc, axis=1, keepdims=True) + EPS) * hn_ref[:, hh * dv:(hh + 1) * dv]
            og = o_ref[pl.ds(r0, L), hh * dv:(hh + 1) * dv].astype(F32)
            y_ref[pl.ds(r0, L), hh * dv:(hh + 1) * dv] = ((0.5 + 0.5 * jnp.tanh(0.5 * og)) * hs).astype(BF16)
        return carry

    lax.fori_loop(0, nc, chunk_step, 0, unroll=MLSTM_CHUNK_UNROLL)


def _mlstm(qkvo, gcol, grow, h_norm, batch, seq, heads_per_step=2):
    T = qkvo.shape[0]
    H, dk, dv = ML_HEADS, ML_DQK, ML_DV
    assert dk == LANES and seq % LANES == 0
    hg = heads_per_step
    nc = seq // LANES
    assert nc % MLSTM_CHUNK_UNROLL == 0
    qb, vb = hg * dk, hg * dv
    k_off = (H * dk) // qb
    v_off = (2 * H * dk) // vb
    o_off = (2 * H * dk + H * dv) // vb
    kern = functools.partial(_mlstm_kernel, heads_per_step=hg, seq=seq)
    factor = pltpu.VMEM((hg, nc, LANES, LANES), F32)
    return pl.pallas_call(
        kern,
        grid=(batch, H // hg),
        in_specs=[pl.BlockSpec((seq, qb), lambda b, g: (b, g)),
                  pl.BlockSpec((seq, qb), lambda b, g: (b, k_off + g)),
                  pl.BlockSpec((seq, vb), lambda b, g: (b, v_off + g)),
                  pl.BlockSpec((seq, vb), lambda b, g: (b, o_off + g)),
                  pl.BlockSpec((seq, LANES), lambda b, g: (b, 0)),
                  pl.BlockSpec((2 * H, seq), lambda b, g: (0, b)),
                  pl.BlockSpec((1, vb), lambda b, g: (0, g))],
        out_specs=pl.BlockSpec((seq, vb), lambda b, g: (b, g)),
        out_shape=jax.ShapeDtypeStruct((T, H * dv), BF16),
        scratch_shapes=[factor, factor, factor, factor,
                        pltpu.VMEM((hg, nc, SUBLANES, LANES), F32),
                        pltpu.VMEM((hg, dk, dv + LANES), F32)],
        compiler_params=_params("arbitrary", "arbitrary"),
        name="mlstm_chunks",
    )(qkvo, qkvo, qkvo, qkvo, gcol, grow, h_norm)


def _outproj_kernel(y_ref, w_ref, x_ref, mod_ref, xo_ref, ho_ref):
    acc = _dot(y_ref[...], w_ref[...])
    x_new = x_ref[...] + mod_ref[0, 2:3, :] * acc
    xo_ref[...] = x_new
    ho_ref[...] = _rms_mod(x_new, mod_ref[0, 4:5, :], mod_ref[0, 3:4, :]).astype(BF16)


def _outproj(y, w, x2d, mod, seq, tm=512):
    T, K = y.shape
    D = w.shape[1]
    per = seq // tm
    return pl.pallas_call(
        _outproj_kernel,
        grid=(T // tm,),
        in_specs=[pl.BlockSpec((tm, K), lambda i: (i, 0)),
                  pl.BlockSpec((K, D), lambda i: (0, 0), pipeline_mode=pl.Buffered(1)),
                  pl.BlockSpec((tm, D), lambda i: (i, 0)),
                  pl.BlockSpec((1, 6, D), lambda i: (i // per, 0, 0))],
        out_specs=[pl.BlockSpec((tm, D), lambda i: (i, 0)),
                   pl.BlockSpec((tm, D), lambda i: (i, 0))],
        out_shape=[jax.ShapeDtypeStruct((T, D), F32),
                   jax.ShapeDtypeStruct((T, D), BF16)],
        compiler_params=_params("arbitrary"),
        name="outproj_residual_norm",
    )(y, w, x2d, mod)


def _ffn_up_kernel(h_ref, wg_ref, wv_ref, cwg_ref, cwv_ref, cbg_ref, cbv_ref, o_ref, ug_ref, uv_ref, *, seq, sub):
    rb, rc = FFN_DOT_ROWS, FFN_EPILOGUE_ROWS
    units = [(s, b0) for s in range(o_ref.shape[1] // sub) for b0 in range(0, seq, rb)]

    def matmul(n, u_ref, w_ref):
        s, b0 = units[n]
        u_ref[n % 2, :SUBLANES, :] = (u_ref[(n - 1) % 2, rb:, :] if b0 else jnp.zeros((SUBLANES, sub), F32))
        u_ref[n % 2, SUBLANES:, :] = _dot(h_ref[b0:b0 + rb, :], w_ref[:, s * sub:(s + 1) * sub].astype(BF16))

    def conv(u_ref, n, r0, cw, cb):
        out = cb + cw[2:3, :] * u_ref[n % 2, SUBLANES + r0:SUBLANES + r0 + rc, :]
        out = out + cw[1:2, :] * u_ref[n % 2, SUBLANES - 1 + r0:SUBLANES - 1 + r0 + rc, :]
        out = out + cw[0:1, :] * u_ref[n % 2, SUBLANES - 2 + r0:SUBLANES - 2 + r0 + rc, :]
        return out

    def epilogue(n, half):
        s, b0 = units[n]
        cols = slice(s * sub, (s + 1) * sub)
        cwg, cwv, cbg, cbv = cwg_ref[:, cols], cwv_ref[:, cols], cbg_ref[:, cols], cbv_ref[:, cols]
        for r0 in range(half * rb // 2, (half + 1) * rb // 2, rc):
            g = conv(ug_ref, n, r0, cwg, cbg)
            v = conv(uv_ref, n, r0, cwv, cbv)
            half_g = 0.5 * g
            o_ref[b0 + r0:b0 + r0 + rc, cols] = ((half_g + half_g * jnp.tanh(half_g)) * v).astype(BF16)

    for n in range(len(units) + 1):
        for half, (u_ref, w_ref) in enumerate(((ug_ref, wg_ref), (uv_ref, wv_ref))):
            if n < len(units):
                matmul(n, u_ref, w_ref)
            if n:
                epilogue(n - 1, half)


def _ffn_up(h, w_up, conv_w, conv_b, seq, tn=512, sub=256):
    T, D = h.shape
    F = w_up.shape[1] // 2
    nf = F // tn
    kern = functools.partial(_ffn_up_kernel, seq=seq, sub=sub)
    cb = conv_b.reshape(1, 2 * F)
    return pl.pallas_call(
        kern,
        grid=(T // seq, nf),
        in_specs=[pl.BlockSpec((seq, D), lambda i, j: (i, 0)),
                  pl.BlockSpec((D, tn), lambda i, j: (0, j)),
                  pl.BlockSpec((D, tn), lambda i, j: (0, nf + j)),
                  pl.BlockSpec((CONV_WIDTH, tn), lambda i, j: (0, j)),
                  pl.BlockSpec((CONV_WIDTH, tn), lambda i, j: (0, nf + j)),
                  pl.BlockSpec((1, tn), lambda i, j: (0, j)),
                  pl.BlockSpec((1, tn), lambda i, j: (0, nf + j))],
        out_specs=pl.BlockSpec((seq, tn), lambda i, j: (i, j)),
        out_shape=jax.ShapeDtypeStruct((T, F), BF16),
        scratch_shapes=[pltpu.VMEM((2, SUBLANES + FFN_DOT_ROWS, sub), F32),
                        pltpu.VMEM((2, SUBLANES + FFN_DOT_ROWS, sub), F32)],
        compiler_params=_params("arbitrary", "arbitrary"),
        name="ffn_up_conv_act",
    )(h, w_up, w_up, conv_w, conv_w, cb, cb)


def _ffn_down_kernel(a_ref, w_ref, x_ref, mod_ref, *rest, emit_norm):
    x_new = x_ref[...] + mod_ref[0, 5:6, :] * _dot(a_ref[...], w_ref[...])
    if emit_norm:
        nmod_ref, xo_ref, ho_ref = rest
        ho_ref[...] = _rms_mod(x_new, nmod_ref[0, 1:2, :], nmod_ref[0, 0:1, :]).astype(BF16)
    else:
        (xo_ref,) = rest
    xo_ref[...] = x_new


def _ffn_down(a, w, x2d, mod, next_mod, seq, tm=256):
    T, K = a.shape
    D = w.shape[1]
    per = seq // tm
    emit_norm = next_mod is not None
    mod_spec = pl.BlockSpec((1, 6, D), lambda i: (i // per, 0, 0))
    row_spec = pl.BlockSpec((tm, D), lambda i: (i, 0))
    in_specs = [pl.BlockSpec((tm, K), lambda i: (i, 0)),
                pl.BlockSpec((K, D), lambda i: (0, 0), pipeline_mode=pl.Buffered(1)),
                row_spec, mod_spec]
    args = [a, w, x2d, mod]
    out_specs = [row_spec]
    out_shape = [jax.ShapeDtypeStruct((T, D), F32)]
    if emit_norm:
        in_specs.append(mod_spec)
        args.append(next_mod)
        out_specs.append(row_spec)
        out_shape.append(jax.ShapeDtypeStruct((T, D), BF16))
    return pl.pallas_call(
        functools.partial(_ffn_down_kernel, emit_norm=emit_norm),
        grid=(T // tm,),
        in_specs=in_specs,
        out_specs=out_specs,
        out_shape=out_shape,
        compiler_params=_params("arbitrary"),
        name="ffn_down_residual_norm",
    )(*args)


def _qkv_kernel(h_ref, wa_ref, wb_ref, ga_ref, gb_ref, blk_ref, o_ref, *, n_q_tiles, sub):
    j = pl.program_id(1)
    h = h_ref[...]
    hd = SW_HEAD_DIM
    half = wa_ref.shape[1]

    def normed(acc, gain):
        ss = _dot((acc * acc).astype(BF16), blk_ref[...])
        return acc * lax.rsqrt(ss * (1.0 / hd) + EPS) * gain

    def duplicated(x):
        lo = lax.broadcasted_iota(jnp.int32, (1, LANES), 1) < hd
        out = []
        for c0 in range(0, x.shape[1], LANES):
            xc = x[:, c0:c0 + LANES]
            rolled = pltpu.roll(xc, hd, axis=1)
            out += [jnp.where(lo, xc, rolled), jnp.where(lo, rolled, xc)]
        return jnp.concatenate(out, axis=1)

    @pl.when(j < n_q_tiles)
    def _():
        srcs = [(w_ref, g_ref, slice(s, s + sub)) for w_ref, g_ref in ((wa_ref, ga_ref), (wb_ref, gb_ref))
                for s in range(0, half, sub)]
        accs = [_dot(h, w_ref[:, cols]) for w_ref, _, cols in srcs]
        for p, (_, g_ref, cols) in enumerate(srcs):
            o_ref[:, p * sub:(p + 1) * sub] = normed(accs[p], g_ref[:, cols]).astype(BF16)

    @pl.when(j == n_q_tiles)
    def _():
        k = _dot(h, wa_ref[:, :sub])
        v = _dot(h, wa_ref[:, sub:])
        o_ref[:, :half] = duplicated(normed(k, ga_ref[:, :sub])).astype(BF16)
        o_ref[:, half:] = duplicated(v).astype(BF16)


def _qkv_proj(h, w, gain_row, tm=2048, tn=1024, sub=256):
    T, D = h.shape
    Hq, Hk, hd = SW_HEADS, SW_KV_HEADS, SW_HEAD_DIM
    assert sub == Hk * hd and (Hq * hd) % tn == 0 and tn == 4 * sub
    n_q_tiles = Hq * hd // tn
    half = tn // 2
    kv_blk = Hq * hd // half
    r = lax.broadcasted_iota(jnp.int32, (sub, sub), 0) // hd
    c = lax.broadcasted_iota(jnp.int32, (sub, sub), 1) // hd
    blk = (r == c).astype(BF16)
    kern = functools.partial(_qkv_kernel, n_q_tiles=n_q_tiles, sub=sub)

    def first(i, j):
        return (0, jnp.minimum(2 * j, kv_blk))

    def second(i, j):
        return (0, jnp.minimum(2 * j + 1, kv_blk))

    return pl.pallas_call(
        kern,
        grid=(T // tm, n_q_tiles + 1),
        in_specs=[pl.BlockSpec((tm, D), lambda i, j: (i, 0)),
                  pl.BlockSpec((D, half), first),
                  pl.BlockSpec((D, half), second),
                  pl.BlockSpec((1, half), first),
                  pl.BlockSpec((1, half), second),
                  pl.BlockSpec((sub, sub), lambda i, j: (0, 0))],
        out_specs=pl.BlockSpec((tm, tn), lambda i, j: (i, j)),
        out_shape=jax.ShapeDtypeStruct((T, (n_q_tiles + 1) * tn), BF16),
        compiler_params=_params("arbitrary", "arbitrary"),
        name="swa_qkv_proj",
    )(h, w, w, gain_row, gain_row, blk)


def _swa_kernel(q_ref, kp_ref, kc_ref, vp_ref, vc_ref, sink_ref, o_ref):
    step = pl.program_id(1)
    BLK, hd = SW_BLOCK, SW_HEAD_DIM
    G = SW_HEADS // SW_KV_HEADS
    pairs = G // 2
    band = 2 * BLK
    nq = q_ref.shape[0] // BLK
    qi = lax.broadcasted_iota(jnp.int32, (BLK, band), 0)
    kr = lax.broadcasted_iota(jnp.int32, (BLK, band), 1)
    rel = BLK + qi - kr
    in_window = (rel >= 0) & (rel < BLK)
    bias_any = jnp.where(in_window, 0.0, -jnp.inf)
    bias_first = jnp.where(in_window & (kr >= jnp.where(step > 0, 0, BLK)), 0.0, -jnp.inf)
    lane = lax.broadcasted_iota(jnp.int32, (1, LANES), 1)
    lo = (lane < hd).astype(BF16)
    hi = (lane >= hd).astype(BF16)
    lo_f = lane < hd
    ones_lo = jnp.broadcast_to(lo, (band, LANES))
    ones_hi = jnp.broadcast_to(hi, (band, LANES))

    def band_rows(prev_ref, cur_ref, qb, cols):
        before = prev_ref[:, cols] if qb == 0 else cur_ref[(qb - 1) * BLK:qb * BLK, cols]
        return jnp.concatenate([before, cur_ref[qb * BLK:(qb + 1) * BLK, cols]], axis=0)

    units = [(qb, kh, p) for qb in range(nq) for kh in range(SW_KV_HEADS) for p in range(pairs)]

    scores = {}
    for qb in range(nq):
        for kh in range(SW_KV_HEADS):
            kd = band_rows(kp_ref, kc_ref, qb, slice(kh * LANES, (kh + 1) * LANES))
            kz = jnp.concatenate([kd * lo, kd * hi], axis=0)
            for p in range(pairs):
                q = q_ref[qb * BLK:(qb + 1) * BLK, (kh * pairs + p) * LANES:(kh * pairs + p + 1) * LANES]
                scores[qb, kh, p] = lax.dot_general(q, kz, (((1,), (1,)), ((), ())), preferred_element_type=F32)

    vz = None
    for qb, kh, p in units:
        if p == 0:
            vd = band_rows(vp_ref, vc_ref, qb, slice(kh * LANES, (kh + 1) * LANES))
            vz = jnp.concatenate([jnp.concatenate([vd * lo, ones_lo], axis=1),
                                  jnp.concatenate([vd * hi, ones_hi], axis=1)], axis=0)
        bias = bias_first if qb == 0 else bias_any
        s = scores[qb, kh, p]
        sink_b = sink_ref[kh, p * BLK:(p + 1) * BLK, :]
        es, mxs = [], []
        for half in range(2):
            sh = s[:, half * band:(half + 1) * band] + bias
            mx = jnp.maximum(jnp.max(sh, axis=1, keepdims=True), sink_b[:, half * hd:half * hd + 1])
            es.append(jnp.exp(sh - mx).astype(BF16))
            mxs.append(mx)
        o2 = _dot(jnp.concatenate(es, axis=1), vz)
        o = o2[:, :LANES] / (o2[:, LANES:] + jnp.exp(sink_b - jnp.where(lo_f, mxs[0], mxs[1])))
        o_ref[qb * BLK:(qb + 1) * BLK, (kh * pairs + p) * LANES:(kh * pairs + p + 1) * LANES] = o.astype(BF16)


def _swa(qkv, sink_cols, batch, seq, q_blocks=4):
    T = qkv.shape[0]
    BLK = SW_BLOCK
    rows = q_blocks * BLK
    ns = seq // rows
    qw = SW_HEADS * SW_HEAD_DIM
    kw = SW_KV_HEADS * LANES
    k_off = qw // kw
    v_off = k_off + 1

    def cur(b, j):
        return b * ns + j

    def prev(b, j):
        return (b * ns + j) * q_blocks - jnp.minimum(j, 1)

    return pl.pallas_call(
        _swa_kernel,
        grid=(batch, ns),
        in_specs=[pl.BlockSpec((rows, qw), lambda b, j: (cur(b, j), 0)),
                  pl.BlockSpec((BLK, kw), lambda b, j: (prev(b, j), k_off)),
                  pl.BlockSpec((rows, kw), lambda b, j: (cur(b, j), k_off)),
                  pl.BlockSpec((BLK, kw), lambda b, j: (prev(b, j), v_off)),
                  pl.BlockSpec((rows, kw), lambda b, j: (cur(b, j), v_off)),
                  pl.BlockSpec(sink_cols.shape, lambda b, j: (0, 0, 0))],
        out_specs=pl.BlockSpec((rows, qw), lambda b, j: (cur(b, j), 0)),
        out_shape=jax.ShapeDtypeStruct((T, qw), BF16),
        compiler_params=_params("arbitrary", "arbitrary"),
        name="swa_attention",
    )(qkv, qkv, qkv, qkv, qkv, sink_cols)


def kernel(x, c, l0_w_mod, l0_b_mod, l0_w_in, l0_gate_bias, l0_h_norm, l0_w_out, l0_w_up, l0_conv_w, l0_conv_b,
           l0_w_down, l1_w_mod, l1_b_mod, l1_w_qkv, l1_q_norm, l1_k_norm, l1_sinks, l1_w_out, l1_w_up, l1_conv_w,
           l1_conv_b, l1_w_down):
    B, S, D = x.shape
    T = B * S
    H, dk, dv = ML_HEADS, ML_DQK, ML_DV
    x2d = x.reshape(T, D)

    mod0 = _modulation(c, l0_w_mod, l0_b_mod).reshape(B, 6, D)
    mod1 = _modulation(c, l1_w_mod, l1_b_mod).reshape(B, 6, D)

    n_main = 2 * H * dk + 2 * H * dv
    w_main = l0_w_in.astype(BF16)
    w_gate = jnp.pad(l0_w_in[:, n_main:], ((0, 0), (0, LANES - 2 * H))).astype(BF16)
    gate_bias = jnp.pad(l0_gate_bias, (0, LANES - 2 * H)).reshape(1, LANES)
    col_scale = jnp.concatenate([jnp.ones((H * dk,), F32), jnp.full((H * dk,), dk ** -0.5, F32),
                                 jnp.ones((2 * H * dv,), F32)]).reshape(1, n_main)
    hm = _first_norm(x2d, mod0, S)
    qkvo, gcol, grow = _inproj(hm, w_main, col_scale, w_gate, gate_bias)
    y = _mlstm(qkvo, gcol, grow, l0_h_norm.reshape(1, H * dv), B, S)
    x1, hf = _outproj(y, l0_w_out.astype(BF16), x2d, mod0, S)
    act = _ffn_up(hf, l0_w_up, l0_conv_w, l0_conv_b, S)
    x2, hm1 = _ffn_down(act, l0_w_down.astype(BF16), x1, mod0, mod1, S)

    Hq, Hk, hd = SW_HEADS, SW_KV_HEADS, SW_HEAD_DIM
    scale = hd ** -0.5

    gain = jnp.concatenate([jnp.tile(l1_q_norm * scale, Hq), jnp.tile(l1_k_norm, Hk),
                            jnp.ones((Hk * hd,), F32)]).reshape(1, -1)
    qkv = _qkv_proj(hm1, l1_w_qkv.astype(BF16), gain)
    pairs = Hq // Hk // 2
    sink_cols = jnp.repeat(jnp.repeat(l1_sinks.reshape(Hk, pairs, 2), SW_BLOCK, axis=1), hd, axis=2)
    att = _swa(qkv, sink_cols, B, S)
    x3, hf1 = _outproj(att, l1_w_out.astype(BF16), x2, mod1, S)
    act1 = _ffn_up(hf1, l1_w_up, l1_conv_w, l1_conv_b, S)
    (x4,) = _ffn_down(act1, l1_w_down.astype(BF16), x3, mod1, None, S)
    return x4.reshape(B, S, D)
```

```python
import functools
import math

import jax
import jax.numpy as jnp
from jax import lax
from jax.experimental import pallas as pl
from jax.experimental.pallas import tpu as pltpu

EPS = 1e-6
ML_HEADS = 8
ML_DQK = 128
ML_DV = 256
GATE_CAP = 15.0
SW_HEADS = 32
SW_KV_HEADS = 4
SW_HEAD_DIM = 64
SW_BLOCK = 128
CONV_WIDTH = 3
MLSTM_CHUNK_UNROLL = 4
FFN_DOT_ROWS = 512
FFN_EPILOGUE_ROWS = 32

LANES = 128
SUBLANES = 8
VMEM_LIMIT = 56 * 1024 * 1024

F32 = jnp.float32
BF16 = jnp.bfloat16


def _params(*sem):
    return pltpu.CompilerParams(dimension_semantics=sem, vmem_limit_bytes=VMEM_LIMIT)


def _dot(a, b):
    return jnp.dot(a, b, preferred_element_type=F32)


def _rms_mod(x, scale_row, shift_row):
    ms = jnp.mean(x * x, axis=-1, keepdims=True)
    return (x * lax.rsqrt(ms + EPS)) * (1.0 + scale_row) + shift_row


def _mod_kernel(c_ref, w_ref, b_ref, o_ref):
    c = c_ref[...]
    sc = (c * jax.nn.sigmoid(c)).astype(BF16)
    o_ref[...] = _dot(sc, w_ref[...].astype(BF16)) + b_ref[...]


def _modulation(c, w_mod, b_mod, tn=1024):
    B, D = c.shape
    N = w_mod.shape[1]
    return pl.pallas_call(
        _mod_kernel,
        grid=(N // tn,),
        in_specs=[pl.BlockSpec((B, D), lambda j: (0, 0)),
                  pl.BlockSpec((D, tn), lambda j: (0, j)),
                  pl.BlockSpec((1, tn), lambda j: (0, j))],
        out_specs=pl.BlockSpec((B, tn), lambda j: (0, j)),
        out_shape=jax.ShapeDtypeStruct((B, N), F32),
        compiler_params=_params("arbitrary"),
        name="modulation",
    )(c, w_mod, b_mod.reshape(1, N))


def _norm_kernel(x_ref, mod_ref, o_ref):
    o_ref[...] = _rms_mod(x_ref[...], mod_ref[0, 1:2, :], mod_ref[0, 0:1, :]).astype(BF16)


def _first_norm(x2d, mod, seq, tm=1024):
    T, D = x2d.shape
    per = seq // tm
    return pl.pallas_call(
        _norm_kernel,
        grid=(T // tm,),
        in_specs=[pl.BlockSpec((tm, D), lambda i: (i, 0)),
                  pl.BlockSpec((1, 6, D), lambda i: (i // per, 0, 0))],
        out_specs=pl.BlockSpec((tm, D), lambda i: (i, 0)),
        out_shape=jax.ShapeDtypeStruct((T, D), BF16),
        compiler_params=_params("arbitrary"),
        name="first_norm",
    )(x2d, mod)


def _inproj_kernel(h_ref, w_ref, cs_ref, wg_ref, gb_ref, o_ref, gcol_ref, grow_ref):
    j = pl.program_id(1)
    h = h_ref[...]
    o_ref[...] = (_dot(h, w_ref[...].astype(BF16)) * cs_ref[...]).astype(BF16)

    @pl.when(j == 0)
    def _():
        g = _dot(h, wg_ref[...]) + gb_ref[...]
        g = GATE_CAP * jnp.tanh(g * (1.0 / GATE_CAP))
        log_f = jnp.minimum(g, 0.0) - jnp.log(1.0 + jnp.exp(-jnp.abs(g)))
        lane = lax.broadcasted_iota(jnp.int32, g.shape, 1)
        g = jnp.where(lane < ML_HEADS, g, log_f)
        gcol_ref[...] = g
        grow_ref[...] = jnp.transpose(g)[: 2 * ML_HEADS, :]


def _inproj(h, w_main, col_scale, w_gate, gate_bias, tm=2048, tn=512):
    T, D = h.shape
    N = col_scale.shape[1]
    return pl.pallas_call(
        _inproj_kernel,
        grid=(T // tm, N // tn),
        in_specs=[pl.BlockSpec((tm, D), lambda i, j: (i, 0)),
                  pl.BlockSpec((D, tn), lambda i, j: (0, j)),
                  pl.BlockSpec((1, tn), lambda i, j: (0, j)),
                  pl.BlockSpec((D, LANES), lambda i, j: (0, 0)),
                  pl.BlockSpec((1, LANES), lambda i, j: (0, 0))],
        out_specs=[pl.BlockSpec((tm, tn), lambda i, j: (i, j)),
                   pl.BlockSpec((tm, LANES), lambda i, j: (i, 0)),
                   pl.BlockSpec((2 * ML_HEADS, tm), lambda i, j: (0, i))],
        out_shape=[jax.ShapeDtypeStruct((T, N), BF16),
                   jax.ShapeDtypeStruct((T, LANES), F32),
                   jax.ShapeDtypeStruct((2 * ML_HEADS, T), F32)],
        compiler_params=_params("arbitrary", "arbitrary"),
        name="mlstm_inproj",
    )(h, w_main, col_scale, w_gate, gate_bias)


def _split_bf16(x):
    hi = x.astype(BF16)
    lo = (x - hi.astype(F32)).astype(BF16)
    return hi, lo


def _mlstm_kernel(q_ref, k_ref, v_ref, o_ref, gcol_ref, grow_ref, hn_ref, y_ref,
                  dexp_ref, wint_ref, floor_ref, wcol_ref, decay_ref, cn_ref,
                  *, heads_per_step, seq):
    L = LANES
    hg = heads_per_step
    nc = seq // L
    dk, dv = ML_DQK, ML_DV
    h0 = pl.program_id(1) * hg
    row = lax.broadcasted_iota(jnp.int32, (L, L), 0)
    col = lax.broadcasted_iota(jnp.int32, (L, L), 1)
    causal = col <= row
    tril = jnp.where(causal, 1.0, 0.0).astype(BF16)
    triu = jnp.where(row <= col, 1.0, 0.0).astype(BF16)

    sel_r = lax.broadcasted_iota(jnp.int32, (L, hg * L), 0)
    sel_head = h0 + lax.broadcasted_iota(jnp.int32, (L, hg * L), 1) // L
    sel_i = jnp.where(sel_r == sel_head, 1.0, 0.0).astype(BF16)
    sel_f = jnp.where(sel_r == sel_head + ML_HEADS, 1.0, 0.0).astype(BF16)
    gate_row = lax.broadcasted_iota(jnp.int32, (2 * ML_HEADS, 1), 0)
    m_run = [jnp.zeros((1, L), F32) for _ in range(hg)]
    spread = []
    for c in range(nc):
        rows = slice(c * L, (c + 1) * L)
        terms = _split_bf16(gcol_ref[rows, :])
        ib_all = sum(_dot(t, sel_i) for t in terms)
        f_terms = [_dot(t, sel_f).astype(BF16) for t in terms]
        grow = grow_ref[:, rows]
        cgrow = sum(_dot(t, triu) for t in _split_bf16(grow))
        spread.append((ib_all, f_terms, grow, cgrow))
    for c in range(nc):
        ib_all, f_terms, grow, cgrow = spread[c]
        bb_all = sum(_dot(tril, t) for t in f_terms)
        for hh in range(hg):
            head = h0 + hh
            m_c = m_run[hh]
            ib, bb = ib_all[:, hh * L:(hh + 1) * L], bb_all[:, hh * L:(hh + 1) * L]
            a_row = jnp.sum(jnp.where(gate_row == head, grow, 0.0)
                            - jnp.where(gate_row == head + ML_HEADS, cgrow, 0.0),
                            axis=0, keepdims=True)
            ca = jnp.max(jnp.where(causal, a_row, -jnp.inf), axis=1, keepdims=True)
            mm = jnp.maximum(m_c, jnp.broadcast_to(ca, (L, L)))
            total = bb[L - 1:L, :]
            m_next = total + mm[L - 1:L, :]
            dexp_ref[hh, c] = jnp.exp(jnp.where(causal, a_row - mm, -jnp.inf))
            wint_ref[hh, c] = jnp.exp(m_c - mm)
            floor_ref[hh, c] = jnp.exp(-(bb + mm))
            wcol_ref[hh, c] = jnp.exp(total + (ib - bb) - m_next)
            decay_ref[hh, c] = jnp.broadcast_to(jnp.exp(total + m_c - m_next), (SUBLANES, L))
            m_run[hh] = m_next

    cn_ref[...] = jnp.zeros_like(cn_ref)
    ones = jnp.ones((L, L), BF16)

    def chunk_step(ci, carry):
        r0 = pl.multiple_of(ci * L, L)
        early = []
        for hh in range(hg):
            q = q_ref[pl.ds(r0, L), hh * dk:(hh + 1) * dk]
            k = k_ref[pl.ds(r0, L), hh * dk:(hh + 1) * dk]
            v = v_ref[pl.ds(r0, L), hh * dv:(hh + 1) * dv]
            v_ext = jnp.concatenate([v, ones], axis=1)
            cn = cn_ref[hh]
            k_w = (k.astype(F32) * wcol_ref[hh, ci]).astype(BF16)
            kv = lax.dot_general(k_w, v_ext, (((0,), (0,)), ((), ())), preferred_element_type=F32)
            s = lax.dot_general(q, k, (((1,), (1,)), ((), ())), preferred_element_type=F32)
            inter = _dot(q, cn.astype(BF16))
            dec = decay_ref[hh, ci][0:1, :]
            cn_ref[hh] = jnp.concatenate([dec] * (cn.shape[1] // L), axis=1) * cn + kv
            early.append((s, inter, v_ext))
        for hh in range(hg):
            s, inter, v_ext = early[hh]
            p = (dexp_ref[hh, ci] * s).astype(BF16)
            intra = _dot(p, v_ext)
            wint = wint_ref[hh, ci]
            num = jnp.concatenate([wint, wint], axis=1) * inter[:, :dv] + intra[:, :dv]
            den = wint * inter[:, dv:] + intra[:, dv:]
            rdn = 1.0 / jnp.maximum(jnp.abs(den), floor_ref[hh, ci])
            hc = num * jnp.concatenate([rdn, rdn], axis=1)
            hs = hc * lax.rsqrt(jnp.mean(hc * hc, axis=1, keepdims=True) + EPS) * hn_ref[:, hh * dv:(hh + 1) * dv]
            og = o_ref[pl.ds(r0, L), hh * dv:(hh + 1) * dv].astype(F32)
            y_ref[pl.ds(r0, L), hh * dv:(hh + 1) * dv] = ((0.5 + 0.5 * jnp.tanh(0.5 * og)) * hs).astype(BF16)
        return carry

    lax.fori_loop(0, nc, chunk_step, 0, unroll=MLSTM_CHUNK_UNROLL)


def _mlstm(qkvo, gcol, grow, h_norm, batch, seq, heads_per_step=2):
    T = qkvo.shape[0]
    H, dk, dv = ML_HEADS, ML_DQK, ML_DV
    assert dk == LANES and seq % LANES == 0
    hg = heads_per_step
    nc = seq // LANES
    assert nc % MLSTM_CHUNK_UNROLL == 0
    qb, vb = hg * dk, hg * dv
    k_off = (H * dk) // qb
    v_off = (2 * H * dk) // vb
    o_off = (2 * H * dk + H * dv) // vb
    kern = functools.partial(_mlstm_kernel, heads_per_step=hg, seq=seq)
    factor = pltpu.VMEM((hg, nc, LANES, LANES), F32)
    return pl.pallas_call(
        kern,
        grid=(batch, H // hg),
        in_specs=[pl.BlockSpec((seq, qb), lambda b, g: (b, g)),
                  pl.BlockSpec((seq, qb), lambda b, g: (b, k_off + g)),
                  pl.BlockSpec((seq, vb), lambda b, g: (b, v_off + g)),
                  pl.BlockSpec((seq, vb), lambda b, g: (b, o_off + g)),
                  pl.BlockSpec((seq, LANES), lambda b, g: (b, 0)),
                  pl.BlockSpec((2 * H, seq), lambda b, g: (0, b)),
                  pl.BlockSpec((1, vb), lambda b, g: (0, g))],
        out_specs=pl.BlockSpec((seq, vb), lambda b, g: (b, g)),
        out_shape=jax.ShapeDtypeStruct((T, H * dv), BF16),
        scratch_shapes=[factor, factor, factor, factor,
                        pltpu.VMEM((hg, nc, SUBLANES, LANES), F32),
                        pltpu.VMEM((hg, dk, dv + LANES), F32)],
        compiler_params=_params("arbitrary", "arbitrary"),
        name="mlstm_chunks",
    )(qkvo, qkvo, qkvo, qkvo, gcol, grow, h_norm)


def _outproj_kernel(y_ref, w_ref, x_ref, mod_ref, xo_ref, ho_ref):
    acc = _dot(y_ref[...], w_ref[...].astype(BF16))
    x_new = x_ref[...] + mod_ref[0, 2:3, :] * acc
    xo_ref[...] = x_new
    ho_ref[...] = _rms_mod(x_new, mod_ref[0, 4:5, :], mod_ref[0, 3:4, :]).astype(BF16)


def _outproj(y, w, x2d, mod, seq, tm=512):
    T, K = y.shape
    D = w.shape[1]
    per = seq // tm
    return pl.pallas_call(
        _outproj_kernel,
        grid=(T // tm,),
        in_specs=[pl.BlockSpec((tm, K), lambda i: (i, 0)),
                  pl.BlockSpec((K, D), lambda i: (0, 0), pipeline_mode=pl.Buffered(1)),
                  pl.BlockSpec((tm, D), lambda i: (i, 0)),
                  pl.BlockSpec((1, 6, D), lambda i: (i // per, 0, 0))],
        out_specs=[pl.BlockSpec((tm, D), lambda i: (i, 0)),
                   pl.BlockSpec((tm, D), lambda i: (i, 0))],
        out_shape=[jax.ShapeDtypeStruct((T, D), F32),
                   jax.ShapeDtypeStruct((T, D), BF16)],
        compiler_params=_params("arbitrary"),
        name="outproj_residual_norm",
    )(y, w, x2d, mod)


def _ffn_up_kernel(h_ref, wg_ref, wv_ref, cwg_ref, cwv_ref, cbg_ref, cbv_ref, o_ref, ug_ref, uv_ref, *, seq, sub):
    rb, rc = FFN_DOT_ROWS, FFN_EPILOGUE_ROWS
    units = [(s, b0) for s in range(o_ref.shape[1] // sub) for b0 in range(0, seq, rb)]

    def matmul(n, u_ref, w_ref):
        s, b0 = units[n]
        u_ref[n % 2, :SUBLANES, :] = (u_ref[(n - 1) % 2, rb:, :] if b0 else jnp.zeros((SUBLANES, sub), F32))
        u_ref[n % 2, SUBLANES:, :] = _dot(h_ref[b0:b0 + rb, :], w_ref[:, s * sub:(s + 1) * sub].astype(BF16))

    def conv(u_ref, n, r0, cw, cb):
        out = cb + cw[2:3, :] * u_ref[n % 2, SUBLANES + r0:SUBLANES + r0 + rc, :]
        out = out + cw[1:2, :] * u_ref[n % 2, SUBLANES - 1 + r0:SUBLANES - 1 + r0 + rc, :]
        out = out + cw[0:1, :] * u_ref[n % 2, SUBLANES - 2 + r0:SUBLANES - 2 + r0 + rc, :]
        return out

    def epilogue(n, half):
        s, b0 = units[n]
        cols = slice(s * sub, (s + 1) * sub)
        cwg, cwv, cbg, cbv = cwg_ref[:, cols], cwv_ref[:, cols], cbg_ref[:, cols], cbv_ref[:, cols]
        for r0 in range(half * rb // 2, (half + 1) * rb // 2, rc):
            g = conv(ug_ref, n, r0, cwg, cbg)
            v = conv(uv_ref, n, r0, cwv, cbv)
            half_g = 0.5 * g
            o_ref[b0 + r0:b0 + r0 + rc, cols] = ((half_g + half_g * jnp.tanh(half_g)) * v).astype(BF16)

    for n in range(len(units) + 1):
        for half, (u_ref, w_ref) in enumerate(((ug_ref, wg_ref), (uv_ref, wv_ref))):
            if n < len(units):
                matmul(n, u_ref, w_ref)
            if n:
                epilogue(n - 1, half)


def _ffn_up(h, w_up, conv_w, conv_b, seq, tn=512, sub=256):
    T, D = h.shape
    F = w_up.shape[1] // 2
    nf = F // tn
    kern = functools.partial(_ffn_up_kernel, seq=seq, sub=sub)
    cb = conv_b.reshape(1, 2 * F)
    return pl.pallas_call(
        kern,
        grid=(T // seq, nf),
        in_specs=[pl.BlockSpec((seq, D), lambda i, j: (i, 0)),
                  pl.BlockSpec((D, tn), lambda i, j: (0, j)),
                  pl.BlockSpec((D, tn), lambda i, j: (0, nf + j)),
                  pl.BlockSpec((CONV_WIDTH, tn), lambda i, j: (0, j)),
                  pl.BlockSpec((CONV_WIDTH, tn), lambda i, j: (0, nf + j)),
                  pl.BlockSpec((1, tn), lambda i, j: (0, j)),
                  pl.BlockSpec((1, tn), lambda i, j: (0, nf + j))],
        out_specs=pl.BlockSpec((seq, tn), lambda i, j: (i, j)),
        out_shape=jax.ShapeDtypeStruct((T, F), BF16),
        scratch_shapes=[pltpu.VMEM((2, SUBLANES + FFN_DOT_ROWS, sub), F32),
                        pltpu.VMEM((2, SUBLANES + FFN_DOT_ROWS, sub), F32)],
        compiler_params=_params("arbitrary", "arbitrary"),
        name="ffn_up_conv_act",
    )(h, w_up, w_up, conv_w, conv_w, cb, cb)


def _ffn_down_kernel(a_ref, w_ref, x_ref, mod_ref, *rest, emit_norm):
    x_new = x_ref[...] + mod_ref[0, 5:6, :] * _dot(a_ref[...], w_ref[...])
    if emit_norm:
        nmod_ref, xo_ref, ho_ref = rest
        ho_ref[...] = _rms_mod(x_new, nmod_ref[0, 1:2, :], nmod_ref[0, 0:1, :]).astype(BF16)
    else:
        (xo_ref,) = rest
    xo_ref[...] = x_new


def _ffn_down(a, w, x2d, mod, next_mod, seq, tm=256):
    T, K = a.shape
    D = w.shape[1]
    per = seq // tm
    emit_norm = next_mod is not None
    mod_spec = pl.BlockSpec((1, 6, D), lambda i: (i // per, 0, 0))
    row_spec = pl.BlockSpec((tm, D), lambda i: (i, 0))
    in_specs = [pl.BlockSpec((tm, K), lambda i: (i, 0)),
                pl.BlockSpec((K, D), lambda i: (0, 0), pipeline_mode=pl.Buffered(1)),
                row_spec, mod_spec]
    args = [a, w, x2d, mod]
    out_specs = [row_spec]
    out_shape = [jax.ShapeDtypeStruct((T, D), F32)]
    if emit_norm:
        in_specs.append(mod_spec)
        args.append(next_mod)
        out_specs.append(row_spec)
        out_shape.append(jax.ShapeDtypeStruct((T, D), BF16))
    return pl.pallas_call(
        functools.partial(_ffn_down_kernel, emit_norm=emit_norm),
        grid=(T // tm,),
        in_specs=in_specs,
        out_specs=out_specs,
        out_shape=out_shape,
        compiler_params=_params("arbitrary"),
        name="ffn_down_residual_norm",
    )(*args)


def _qkv_kernel(h_ref, wa_ref, wb_ref, ga_ref, gb_ref, blk_ref, o_ref, *, n_q_tiles, sub):
    j = pl.program_id(1)
    h = h_ref[...]
    hd = SW_HEAD_DIM
    half = wa_ref.shape[1]

    def normed(acc, gain):
        ss = _dot((acc * acc).astype(BF16), blk_ref[...])
        return acc * lax.rsqrt(ss * (1.0 / hd) + EPS) * gain

    def duplicated(x):
        lo = lax.broadcasted_iota(jnp.int32, (1, LANES), 1) < hd
        out = []
        for c0 in range(0, x.shape[1], LANES):
            xc = x[:, c0:c0 + LANES]
            rolled = pltpu.roll(xc, hd, axis=1)
            out += [jnp.where(lo, xc, rolled), jnp.where(lo, rolled, xc)]
        return jnp.concatenate(out, axis=1)

    @pl.when(j < n_q_tiles)
    def _():
        srcs = [(w_ref, g_ref, slice(s, s + sub)) for w_ref, g_ref in ((wa_ref, ga_ref), (wb_ref, gb_ref))
                for s in range(0, half, sub)]
        accs = [_dot(h, w_ref[:, cols]) for w_ref, _, cols in srcs]
        for p, (_, g_ref, cols) in enumerate(srcs):
            o_ref[:, p * sub:(p + 1) * sub] = normed(accs[p], g_ref[:, cols]).astype(BF16)

    @pl.when(j == n_q_tiles)
    def _():
        k = _dot(h, wa_ref[:, :sub])
        v = _dot(h, wa_ref[:, sub:])
        o_ref[:, :half] = duplicated(normed(k, ga_ref[:, :sub])).astype(BF16)
        o_ref[:, half:] = duplicated(v).astype(BF16)


def _qkv_proj(h, w, gain_row, tm=2048, tn=1024, sub=256):
    T, D = h.shape
    Hq, Hk, hd = SW_HEADS, SW_KV_HEADS, SW_HEAD_DIM
    assert sub == Hk * hd and (Hq * hd) % tn == 0 and tn == 4 * sub
    n_q_tiles = Hq * hd // tn
    half = tn // 2
    kv_blk = Hq * hd // half
    r = lax.broadcasted_iota(jnp.int32, (sub, sub), 0) // hd
    c = lax.broadcasted_iota(jnp.int32, (sub, sub), 1) // hd
    blk = (r == c).astype(BF16)
    kern = functools.partial(_qkv_kernel, n_q_tiles=n_q_tiles, sub=sub)

    def first(i, j):
        return (0, jnp.minimum(2 * j, kv_blk))

    def second(i, j):
        return (0, jnp.minimum(2 * j + 1, kv_blk))

    return pl.pallas_call(
        kern,
        grid=(T // tm, n_q_tiles + 1),
        in_specs=[pl.BlockSpec((tm, D), lambda i, j: (i, 0)),
                  pl.BlockSpec((D, half), first),
                  pl.BlockSpec((D, half), second),
                  pl.BlockSpec((1, half), first),
                  pl.BlockSpec((1, half), second),
                  pl.BlockSpec((sub, sub), lambda i, j: (0, 0))],
        out_specs=pl.BlockSpec((tm, tn), lambda i, j: (i, j)),
        out_shape=jax.ShapeDtypeStruct((T, (n_q_tiles + 1) * tn), BF16),
        compiler_params=_params("arbitrary", "arbitrary"),
        name="swa_qkv_proj",
    )(h, w, w, gain_row, gain_row, blk)


def _swa_kernel(q_ref, kp_ref, kc_ref, vp_ref, vc_ref, sink_ref, o_ref):
    step = pl.program_id(1)
    BLK, hd = SW_BLOCK, SW_HEAD_DIM
    G = SW_HEADS // SW_KV_HEADS
    pairs = G // 2
    band = 2 * BLK
    nq = q_ref.shape[0] // BLK
    qi = lax.broadcasted_iota(jnp.int32, (BLK, band), 0)
    kr = lax.broadcasted_iota(jnp.int32, (BLK, band), 1)
    rel = BLK + qi - kr
    in_window = (rel >= 0) & (rel < BLK)
    bias_any = jnp.where(in_window, 0.0, -jnp.inf)
    bias_first = jnp.where(in_window & (kr >= jnp.where(step > 0, 0, BLK)), 0.0, -jnp.inf)
    lane = lax.broadcasted_iota(jnp.int32, (1, LANES), 1)
    lo = (lane < hd).astype(BF16)
    hi = (lane >= hd).astype(BF16)
    lo_f = lane < hd
    ones_lo = jnp.broadcast_to(lo, (band, LANES))
    ones_hi = jnp.broadcast_to(hi, (band, LANES))

    def band_rows(prev_ref, cur_ref, qb, cols):
        before = prev_ref[:, cols] if qb == 0 else cur_ref[(qb - 1) * BLK:qb * BLK, cols]
        return jnp.concatenate([before, cur_ref[qb * BLK:(qb + 1) * BLK, cols]], axis=0)

    units = [(qb, kh, p) for qb in range(nq) for kh in range(SW_KV_HEADS) for p in range(pairs)]

    scores = {}
    for qb in range(nq):
        for kh in range(SW_KV_HEADS):
            kd = band_rows(kp_ref, kc_ref, qb, slice(kh * LANES, (kh + 1) * LANES))
            kz = jnp.concatenate([kd * lo, kd * hi], axis=0)
            for p in range(pairs):
                q = q_ref[qb * BLK:(qb + 1) * BLK, (kh * pairs + p) * LANES:(kh * pairs + p + 1) * LANES]
                scores[qb, kh, p] = lax.dot_general(q, kz, (((1,), (1,)), ((), ())), preferred_element_type=F32)

    vz = None
    for qb, kh, p in units:
        if p == 0:
            vd = band_rows(vp_ref, vc_ref, qb, slice(kh * LANES, (kh + 1) * LANES))
            vz = jnp.concatenate([jnp.concatenate([vd * lo, ones_lo], axis=1),
                                  jnp.concatenate([vd * hi, ones_hi], axis=1)], axis=0)
        bias = bias_first if qb == 0 else bias_any
        s = scores[qb, kh, p]
        sink_b = sink_ref[kh, p * BLK:(p + 1) * BLK, :]
        es, mxs = [], []
        for half in range(2):
            sh = s[:, half * band:(half + 1) * band] + bias
            mx = jnp.maximum(jnp.max(sh, axis=1, keepdims=True), sink_b[:, half * hd:half * hd + 1])
            es.append(jnp.exp(sh - mx).astype(BF16))
            mxs.append(mx)
        o2 = _dot(jnp.concatenate(es, axis=1), vz)
        o = o2[:, :LANES] / (o2[:, LANES:] + jnp.exp(sink_b - jnp.where(lo_f, mxs[0], mxs[1])))
        o_ref[qb * BLK:(qb + 1) * BLK, (kh * pairs + p) * LANES:(kh * pairs + p + 1) * LANES] = o.astype(BF16)


def _swa(qkv, sink_cols, batch, seq, q_blocks=4):
    T = qkv.shape[0]
    BLK = SW_BLOCK
    rows = q_blocks * BLK
    ns = seq // rows
    qw = SW_HEADS * SW_HEAD_DIM
    kw = SW_KV_HEADS * LANES
    k_off = qw // kw
    v_off = k_off + 1

    def cur(b, j):
        return b * ns + j

    def prev(b, j):
        return (b * ns + j) * q_blocks - jnp.minimum(j, 1)

    return pl.pallas_call(
        _swa_kernel,
        grid=(batch, ns),
        in_specs=[pl.BlockSpec((rows, qw), lambda b, j: (cur(b, j), 0)),
                  pl.BlockSpec((BLK, kw), lambda b, j: (prev(b, j), k_off)),
                  pl.BlockSpec((rows, kw), lambda b, j: (cur(b, j), k_off)),
                  pl.BlockSpec((BLK, kw), lambda b, j: (prev(b, j), v_off)),
                  pl.BlockSpec((rows, kw), lambda b, j: (cur(b, j), v_off)),
                  pl.BlockSpec(sink_cols.shape, lambda b, j: (0, 0, 0))],
        out_specs=pl.BlockSpec((rows, qw), lambda b, j: (cur(b, j), 0)),
        out_shape=jax.ShapeDtypeStruct((T, qw), BF16),
        compiler_params=_params("arbitrary", "arbitrary"),
        name="swa_attention",
    )(qkv, qkv, qkv, qkv, qkv, sink_cols)


def kernel(x, c, l0_w_mod, l0_b_mod, l0_w_in, l0_gate_bias, l0_h_norm, l0_w_out, l0_w_up, l0_conv_w, l0_conv_b,
           l0_w_down, l1_w_mod, l1_b_mod, l1_w_qkv, l1_q_norm, l1_k_norm, l1_sinks, l1_w_out, l1_w_up, l1_conv_w,
           l1_conv_b, l1_w_down):
    B, S, D = x.shape
    T = B * S
    H, dk, dv = ML_HEADS, ML_DQK, ML_DV
    x2d = x.reshape(T, D)

    mod0 = _modulation(c, l0_w_mod, l0_b_mod).reshape(B, 6, D)
    mod1 = _modulation(c, l1_w_mod, l1_b_mod).reshape(B, 6, D)

    n_main = 2 * H * dk + 2 * H * dv
    w_main = l0_w_in
    w_gate = jnp.pad(l0_w_in[:, n_main:], ((0, 0), (0, LANES - 2 * H))).astype(BF16)
    gate_bias = jnp.pad(l0_gate_bias, (0, LANES - 2 * H)).reshape(1, LANES)
    col_scale = jnp.concatenate([jnp.ones((H * dk,), F32), jnp.full((H * dk,), dk ** -0.5, F32),
                                 jnp.ones((2 * H * dv,), F32)]).reshape(1, n_main)
    hm = _first_norm(x2d, mod0, S)
    qkvo, gcol, grow = _inproj(hm, w_main, col_scale, w_gate, gate_bias)
    y = _mlstm(qkvo, gcol, grow, l0_h_norm.reshape(1, H * dv), B, S)
    x1, hf = _outproj(y, l0_w_out, x2d, mod0, S)
    act = _ffn_up(hf, l0_w_up, l0_conv_w, l0_conv_b, S)
    x2, hm1 = _ffn_down(act, l0_w_down.astype(BF16), x1, mod0, mod1, S)

    Hq, Hk, hd = SW_HEADS, SW_KV_HEADS, SW_HEAD_DIM
    scale = hd ** -0.5

    gain = jnp.concatenate([jnp.tile(l1_q_norm * scale, Hq), jnp.tile(l1_k_norm, Hk),
                            jnp.ones((Hk * hd,), F32)]).reshape(1, -1)
    qkv = _qkv_proj(hm1, l1_w_qkv.astype(BF16), gain)
    pairs = Hq // Hk // 2
    sink_cols = jnp.repeat(jnp.repeat(l1_sinks.reshape(Hk, pairs, 2), SW_BLOCK, axis=1), hd, axis=2)
    att = _swa(qkv, sink_cols, B, S)
    x3, hf1 = _outproj(att, l1_w_out, x2, mod1, S)
    act1 = _ffn_up(hf1, l1_w_up, l1_conv_w, l1_conv_b, S)
    (x4,) = _ffn_down(act1, l1_w_down.astype(BF16), x3, mod1, None, S)
    return x4.reshape(B, S, D)
```

```python
import functools
import math

import jax
import jax.numpy as jnp
from jax import lax
from jax.experimental import pallas as pl
from jax.experimental.pallas import tpu as pltpu

EPS = 1e-6
ML_HEADS = 8
ML_DQK = 128
ML_DV = 256
GATE_CAP = 15.0
SW_HEADS = 32
SW_KV_HEADS = 4
SW_HEAD_DIM = 64
SW_BLOCK = 128
CONV_WIDTH = 3
MLSTM_CHUNK_UNROLL = 4
FFN_DOT_ROWS = 512
FFN_EPILOGUE_ROWS = 32

LANES = 128
SUBLANES = 8
VMEM_LIMIT = 56 * 1024 * 1024

F32 = jnp.float32
BF16 = jnp.bfloat16


def _params(*sem):
    return pltpu.CompilerParams(dimension_semantics=sem, vmem_limit_bytes=VMEM_LIMIT)


def _dot(a, b):
    return jnp.dot(a, b, preferred_element_type=F32)


def _rms_mod(x, scale_row, shift_row):
    ms = jnp.mean(x * x, axis=-1, keepdims=True)
    return (x * lax.rsqrt(ms + EPS)) * (1.0 + scale_row) + shift_row


def _mod_kernel(c_ref, w_ref, b_ref, o_ref):
    c = c_ref[...]
    sc = (c * jax.nn.sigmoid(c)).astype(BF16)
    o_ref[...] = _dot(sc, w_ref[...].astype(BF16)) + b_ref[...]


def _modulation(c, w_mod, b_mod, tn=1024):
    B, D = c.shape
    N = w_mod.shape[1]
    return pl.pallas_call(
        _mod_kernel,
        grid=(N // tn,),
        in_specs=[pl.BlockSpec((B, D), lambda j: (0, 0)),
                  pl.BlockSpec((D, tn), lambda j: (0, j)),
                  pl.BlockSpec((1, tn), lambda j: (0, j))],
        out_specs=pl.BlockSpec((B, tn), lambda j: (0, j)),
        out_shape=jax.ShapeDtypeStruct((B, N), F32),
        compiler_params=_params("arbitrary"),
        name="modulation",
    )(c, w_mod, b_mod.reshape(1, N))


def _norm_kernel(x_ref, mod_ref, o_ref):
    o_ref[...] = _rms_mod(x_ref[...], mod_ref[0, 1:2, :], mod_ref[0, 0:1, :]).astype(BF16)


def _first_norm(x2d, mod, seq, tm=1024):
    T, D = x2d.shape
    per = seq // tm
    return pl.pallas_call(
        _norm_kernel,
        grid=(T // tm,),
        in_specs=[pl.BlockSpec((tm, D), lambda i: (i, 0)),
                  pl.BlockSpec((1, 6, D), lambda i: (i // per, 0, 0))],
        out_specs=pl.BlockSpec((tm, D), lambda i: (i, 0)),
        out_shape=jax.ShapeDtypeStruct((T, D), BF16),
        compiler_params=_params("arbitrary"),
        name="first_norm",
    )(x2d, mod)


def _inproj_kernel(h_ref, w_ref, cs_ref, wg_ref, gb_ref, o_ref, gcol_ref, grow_ref):
    j = pl.program_id(1)
    h = h_ref[...]
    o_ref[...] = (_dot(h, w_ref[...]) * cs_ref[...]).astype(BF16)

    @pl.when(j == 0)
    def _():
        g = _dot(h, wg_ref[...]) + gb_ref[...]
        g = GATE_CAP * jnp.tanh(g * (1.0 / GATE_CAP))
        log_f = jnp.minimum(g, 0.0) - jnp.log(1.0 + jnp.exp(-jnp.abs(g)))
        lane = lax.broadcasted_iota(jnp.int32, g.shape, 1)
        g = jnp.where(lane < ML_HEADS, g, log_f)
        gcol_ref[...] = g
        grow_ref[...] = jnp.transpose(g)[: 2 * ML_HEADS, :]


def _inproj(h, w_main, col_scale, w_gate, gate_bias, tm=2048, tn=1024):
    T, D = h.shape
    N = col_scale.shape[1]
    return pl.pallas_call(
        _inproj_kernel,
        grid=(T // tm, N // tn),
        in_specs=[pl.BlockSpec((tm, D), lambda i, j: (i, 0)),
                  pl.BlockSpec((D, tn), lambda i, j: (0, j)),
                  pl.BlockSpec((1, tn), lambda i, j: (0, j)),
                  pl.BlockSpec((D, LANES), lambda i, j: (0, 0)),
                  pl.BlockSpec((1, LANES), lambda i, j: (0, 0))],
        out_specs=[pl.BlockSpec((tm, tn), lambda i, j: (i, j)),
                   pl.BlockSpec((tm, LANES), lambda i, j: (i, 0)),
                   pl.BlockSpec((2 * ML_HEADS, tm), lambda i, j: (0, i))],
        out_shape=[jax.ShapeDtypeStruct((T, N), BF16),
                   jax.ShapeDtypeStruct((T, LANES), F32),
                   jax.ShapeDtypeStruct((2 * ML_HEADS, T), F32)],
        compiler_params=_params("arbitrary", "arbitrary"),
        name="mlstm_inproj",
    )(h, w_main, col_scale, w_gate, gate_bias)


def _split_bf16(x):
    hi = x.astype(BF16)
    lo = (x - hi.astype(F32)).astype(BF16)
    return hi, lo


def _mlstm_kernel(q_ref, k_ref, v_ref, o_ref, gcol_ref, grow_ref, hn_ref, y_ref,
                  dexp_ref, wint_ref, floor_ref, wcol_ref, decay_ref, cn_ref,
                  *, heads_per_step, seq):
    L = LANES
    hg = heads_per_step
    nc = seq // L
    dk, dv = ML_DQK, ML_DV
    h0 = pl.program_id(1) * hg
    row = lax.broadcasted_iota(jnp.int32, (L, L), 0)
    col = lax.broadcasted_iota(jnp.int32, (L, L), 1)
    causal = col <= row
    tril = jnp.where(causal, 1.0, 0.0).astype(BF16)
    triu = jnp.where(row <= col, 1.0, 0.0).astype(BF16)

    sel_r = lax.broadcasted_iota(jnp.int32, (L, hg * L), 0)
    sel_head = h0 + lax.broadcasted_iota(jnp.int32, (L, hg * L), 1) // L
    sel_i = jnp.where(sel_r == sel_head, 1.0, 0.0).astype(BF16)
    sel_f = jnp.where(sel_r == sel_head + ML_HEADS, 1.0, 0.0).astype(BF16)
    gate_row = lax.broadcasted_iota(jnp.int32, (2 * ML_HEADS, 1), 0)
    m_run = [jnp.zeros((1, L), F32) for _ in range(hg)]
    spread = []
    for c in range(nc):
        rows = slice(c * L, (c + 1) * L)
        terms = _split_bf16(gcol_ref[rows, :])
        ib_all = sum(_dot(t, sel_i) for t in terms)
        f_terms = [_dot(t, sel_f).astype(BF16) for t in terms]
        grow = grow_ref[:, rows]
        cgrow = sum(_dot(t, triu) for t in _split_bf16(grow))
        spread.append((ib_all, f_terms, grow, cgrow))
    for c in range(nc):
        ib_all, f_terms, grow, cgrow = spread[c]
        bb_all = sum(_dot(tril, t) for t in f_terms)
        for hh in range(hg):
            head = h0 + hh
            m_c = m_run[hh]
            ib, bb = ib_all[:, hh * L:(hh + 1) * L], bb_all[:, hh * L:(hh + 1) * L]
            a_row = jnp.sum(jnp.where(gate_row == head, grow, 0.0)
                            - jnp.where(gate_row == head + ML_HEADS, cgrow, 0.0),
                            axis=0, keepdims=True)
            ca = jnp.max(jnp.where(causal, a_row, -jnp.inf), axis=1, keepdims=True)
            mm = jnp.maximum(m_c, jnp.broadcast_to(ca, (L, L)))
            total = bb[L - 1:L, :]
            m_next = total + mm[L - 1:L, :]
            dexp_ref[hh, c] = jnp.exp(jnp.where(causal, a_row - mm, -jnp.inf))
            wint_ref[hh, c] = jnp.exp(m_c - mm)
            floor_ref[hh, c] = jnp.exp(-(bb + mm))
            wcol_ref[hh, c] = jnp.exp(total + (ib - bb) - m_next)
            decay_ref[hh, c] = jnp.broadcast_to(jnp.exp(total + m_c - m_next), (SUBLANES, L))
            m_run[hh] = m_next

    cn_ref[...] = jnp.zeros_like(cn_ref)
    ones = jnp.ones((L, L), BF16)

    def chunk_step(ci, carry):
        r0 = pl.multiple_of(ci * L, L)
        early = []
        for hh in range(hg):
            q = q_ref[pl.ds(r0, L), hh * dk:(hh + 1) * dk]
            k = k_ref[pl.ds(r0, L), hh * dk:(hh + 1) * dk]
            v = v_ref[pl.ds(r0, L), hh * dv:(hh + 1) * dv]
            v_ext = jnp.concatenate([v, ones], axis=1)
            cn = cn_ref[hh]
            k_w = (k.astype(F32) * wcol_ref[hh, ci]).astype(BF16)
            kv = lax.dot_general(k_w, v_ext, (((0,), (0,)), ((), ())), preferred_element_type=F32)
            s = lax.dot_general(q, k, (((1,), (1,)), ((), ())), preferred_element_type=F32)
            inter = _dot(q, cn.astype(BF16))
            dec = decay_ref[hh, ci][0:1, :]
            cn_ref[hh] = jnp.concatenate([dec] * (cn.shape[1] // L), axis=1) * cn + kv
            early.append((s, inter, v_ext))
        for hh in range(hg):
            s, inter, v_ext = early[hh]
            p = (dexp_ref[hh, ci] * s).astype(BF16)
            intra = _dot(p, v_ext)
            wint = wint_ref[hh, ci]
            num = jnp.concatenate([wint, wint], axis=1) * inter[:, :dv] + intra[:, :dv]
            den = wint * inter[:, dv:] + intra[:, dv:]
            rdn = 1.0 / jnp.maximum(jnp.abs(den), floor_ref[hh, ci])
            hc = num * jnp.concatenate([rdn, rdn], axis=1)
            hs = hc * lax.rsqrt(jnp.mean(hc * hc, axis=1, keepdims=True) + EPS) * hn_ref[:, hh * dv:(hh + 1) * dv]
            og = o_ref[pl.ds(r0, L), hh * dv:(hh + 1) * dv].astype(F32)
            y_ref[pl.ds(r0, L), hh * dv:(hh + 1) * dv] = ((0.5 + 0.5 * jnp.tanh(0.5 * og)) * hs).astype(BF16)
        return carry

    lax.fori_loop(0, nc, chunk_step, 0, unroll=MLSTM_CHUNK_UNROLL)


def _mlstm(qkvo, gcol, grow, h_norm, batch, seq, heads_per_step=2):
    T = qkvo.shape[0]
    H, dk, dv = ML_HEADS, ML_DQK, ML_DV
    assert dk == LANES and seq % LANES == 0
    hg = heads_per_step
    nc = seq // LANES
    assert nc % MLSTM_CHUNK_UNROLL == 0
    qb, vb = hg * dk, hg * dv
    k_off = (H * dk) // qb
    v_off = (2 * H * dk) // vb
    o_off = (2 * H * dk + H * dv) // vb
    kern = functools.partial(_mlstm_kernel, heads_per_step=hg, seq=seq)
    factor = pltpu.VMEM((hg, nc, LANES, LANES), F32)
    return pl.pallas_call(
        kern,
        grid=(batch, H // hg),
        in_specs=[pl.BlockSpec((seq, qb), lambda b, g: (b, g)),
                  pl.BlockSpec((seq, qb), lambda b, g: (b, k_off + g)),
                  pl.BlockSpec((seq, vb), lambda b, g: (b, v_off + g)),
                  pl.BlockSpec((seq, vb), lambda b, g: (b, o_off + g)),
                  pl.BlockSpec((seq, LANES), lambda b, g: (b, 0)),
                  pl.BlockSpec((2 * H, seq), lambda b, g: (0, b)),
                  pl.BlockSpec((1, vb), lambda b, g: (0, g))],
        out_specs=pl.BlockSpec((seq, vb), lambda b, g: (b, g)),
        out_shape=jax.ShapeDtypeStruct((T, H * dv), BF16),
        scratch_shapes=[factor, factor, factor, factor,
                        pltpu.VMEM((hg, nc, SUBLANES, LANES), F32),
                        pltpu.VMEM((hg, dk, dv + LANES), F32)],
        compiler_params=_params("arbitrary", "arbitrary"),
        name="mlstm_chunks",
    )(qkvo, qkvo, qkvo, qkvo, gcol, grow, h_norm)


def _outproj_kernel(y_ref, w_ref, x_ref, mod_ref, xo_ref, ho_ref):
    acc = _dot(y_ref[...], w_ref[...].astype(BF16))
    x_new = x_ref[...] + mod_ref[0, 2:3, :] * acc
    xo_ref[...] = x_new
    ho_ref[...] = _rms_mod(x_new, mod_ref[0, 4:5, :], mod_ref[0, 3:4, :]).astype(BF16)


def _outproj(y, w, x2d, mod, seq, tm=512):
    T, K = y.shape
    D = w.shape[1]
    per = seq // tm
    return pl.pallas_call(
        _outproj_kernel,
        grid=(T // tm,),
        in_specs=[pl.BlockSpec((tm, K), lambda i: (i, 0)),
                  pl.BlockSpec((K, D), lambda i: (0, 0), pipeline_mode=pl.Buffered(1)),
                  pl.BlockSpec((tm, D), lambda i: (i, 0)),
                  pl.BlockSpec((1, 6, D), lambda i: (i // per, 0, 0))],
        out_specs=[pl.BlockSpec((tm, D), lambda i: (i, 0)),
                   pl.BlockSpec((tm, D), lambda i: (i, 0))],
        out_shape=[jax.ShapeDtypeStruct((T, D), F32),
                   jax.ShapeDtypeStruct((T, D), BF16)],
        compiler_params=_params("arbitrary"),
        name="outproj_residual_norm",
    )(y, w, x2d, mod)


def _ffn_up_kernel(h_ref, wg_ref, wv_ref, cwg_ref, cwv_ref, cbg_ref, cbv_ref, o_ref, ug_ref, uv_ref, *, seq, sub):
    rb, rc = FFN_DOT_ROWS, FFN_EPILOGUE_ROWS
    units = [(s, b0) for s in range(o_ref.shape[1] // sub) for b0 in range(0, seq, rb)]

    def matmul(n, u_ref, w_ref):
        s, b0 = units[n]
        u_ref[n % 2, :SUBLANES, :] = (u_ref[(n - 1) % 2, rb:, :] if b0 else jnp.zeros((SUBLANES, sub), F32))
        u_ref[n % 2, SUBLANES:, :] = _dot(h_ref[b0:b0 + rb, :], w_ref[:, s * sub:(s + 1) * sub].astype(BF16))

    def conv(u_ref, n, r0, cw, cb):
        out = cb + cw[2:3, :] * u_ref[n % 2, SUBLANES + r0:SUBLANES + r0 + rc, :]
        out = out + cw[1:2, :] * u_ref[n % 2, SUBLANES - 1 + r0:SUBLANES - 1 + r0 + rc, :]
        out = out + cw[0:1, :] * u_ref[n % 2, SUBLANES - 2 + r0:SUBLANES - 2 + r0 + rc, :]
        return out

    def epilogue(n, half):
        s, b0 = units[n]
        cols = slice(s * sub, (s + 1) * sub)
        cwg, cwv, cbg, cbv = cwg_ref[:, cols], cwv_ref[:, cols], cbg_ref[:, cols], cbv_ref[:, cols]
        for r0 in range(half * rb // 2, (half + 1) * rb // 2, rc):
            g = conv(ug_ref, n, r0, cwg, cbg)
            v = conv(uv_ref, n, r0, cwv, cbv)
            half_g = 0.5 * g
            o_ref[b0 + r0:b0 + r0 + rc, cols] = ((half_g + half_g * jnp.tanh(half_g)) * v).astype(BF16)

    for n in range(len(units) + 1):
        for half, (u_ref, w_ref) in enumerate(((ug_ref, wg_ref), (uv_ref, wv_ref))):
            if n < len(units):
                matmul(n, u_ref, w_ref)
            if n:
                epilogue(n - 1, half)


def _ffn_up(h, w_up, conv_w, conv_b, seq, tn=512, sub=256):
    T, D = h.shape
    F = w_up.shape[1] // 2
    nf = F // tn
    kern = functools.partial(_ffn_up_kernel, seq=seq, sub=sub)
    cb = conv_b.reshape(1, 2 * F)
    return pl.pallas_call(
        kern,
        grid=(T // seq, nf),
        in_specs=[pl.BlockSpec((seq, D), lambda i, j: (i, 0)),
                  pl.BlockSpec((D, tn), lambda i, j: (0, j)),
                  pl.BlockSpec((D, tn), lambda i, j: (0, nf + j)),
                  pl.BlockSpec((CONV_WIDTH, tn), lambda i, j: (0, j)),
                  pl.BlockSpec((CONV_WIDTH, tn), lambda i, j: (0, nf + j)),
                  pl.BlockSpec((1, tn), lambda i, j: (0, j)),
                  pl.BlockSpec((1, tn), lambda i, j: (0, nf + j))],
        out_specs=pl.BlockSpec((seq, tn), lambda i, j: (i, j)),
        out_shape=jax.ShapeDtypeStruct((T, F), BF16),
        scratch_shapes=[pltpu.VMEM((2, SUBLANES + FFN_DOT_ROWS, sub), F32),
                        pltpu.VMEM((2, SUBLANES + FFN_DOT_ROWS, sub), F32)],
        compiler_params=_params("arbitrary", "arbitrary"),
        name="ffn_up_conv_act",
    )(h, w_up, w_up, conv_w, conv_w, cb, cb)


def _ffn_down_kernel(a_ref, w_ref, x_ref, mod_ref, *rest, emit_norm):
    x_new = x_ref[...] + mod_ref[0, 5:6, :] * _dot(a_ref[...], w_ref[...])
    if emit_norm:
        nmod_ref, xo_ref, ho_ref = rest
        ho_ref[...] = _rms_mod(x_new, nmod_ref[0, 1:2, :], nmod_ref[0, 0:1, :]).astype(BF16)
    else:
        (xo_ref,) = rest
    xo_ref[...] = x_new


def _ffn_down(a, w, x2d, mod, next_mod, seq, tm=256):
    T, K = a.shape
    D = w.shape[1]
    per = seq // tm
    emit_norm = next_mod is not None
    mod_spec = pl.BlockSpec((1, 6, D), lambda i: (i // per, 0, 0))
    row_spec = pl.BlockSpec((tm, D), lambda i: (i, 0))
    in_specs = [pl.BlockSpec((tm, K), lambda i: (i, 0)),
                pl.BlockSpec((K, D), lambda i: (0, 0), pipeline_mode=pl.Buffered(1)),
                row_spec, mod_spec]
    args = [a, w, x2d, mod]
    out_specs = [row_spec]
    out_shape = [jax.ShapeDtypeStruct((T, D), F32)]
    if emit_norm:
        in_specs.append(mod_spec)
        args.append(next_mod)
        out_specs.append(row_spec)
        out_shape.append(jax.ShapeDtypeStruct((T, D), BF16))
    return pl.pallas_call(
        functools.partial(_ffn_down_kernel, emit_norm=emit_norm),
        grid=(T // tm,),
        in_specs=in_specs,
        out_specs=out_specs,
        out_shape=out_shape,
        compiler_params=_params("arbitrary"),
        name="ffn_down_residual_norm",
    )(*args)


def _qkv_kernel(h_ref, wa_ref, wb_ref, ga_ref, gb_ref, blk_ref, o_ref, *, n_q_tiles, sub):
    j = pl.program_id(1)
    h = h_ref[...]
    hd = SW_HEAD_DIM
    half = wa_ref.shape[1]

    def normed(acc, gain):
        ss = _dot((acc * acc).astype(BF16), blk_ref[...])
        return acc * lax.rsqrt(ss * (1.0 / hd) + EPS) * gain

    def duplicated(x):
        lo = lax.broadcasted_iota(jnp.int32, (1, LANES), 1) < hd
        out = []
        for c0 in range(0, x.shape[1], LANES):
            xc = x[:, c0:c0 + LANES]
            rolled = pltpu.roll(xc, hd, axis=1)
            out += [jnp.where(lo, xc, rolled), jnp.where(lo, rolled, xc)]
        return jnp.concatenate(out, axis=1)

    @pl.when(j < n_q_tiles)
    def _():
        srcs = [(w_ref, g_ref, slice(s, s + sub)) for w_ref, g_ref in ((wa_ref, ga_ref), (wb_ref, gb_ref))
                for s in range(0, half, sub)]
        accs = [_dot(h, w_ref[:, cols]) for w_ref, _, cols in srcs]
        for p, (_, g_ref, cols) in enumerate(srcs):
            o_ref[:, p * sub:(p + 1) * sub] = normed(accs[p], g_ref[:, cols]).astype(BF16)

    @pl.when(j == n_q_tiles)
    def _():
        k = _dot(h, wa_ref[:, :sub])
        v = _dot(h, wa_ref[:, sub:])
        o_ref[:, :half] = duplicated(normed(k, ga_ref[:, :sub])).astype(BF16)
        o_ref[:, half:] = duplicated(v).astype(BF16)


def _qkv_proj(h, w, gain_row, tm=2048, tn=1024, sub=256):
    T, D = h.shape
    Hq, Hk, hd = SW_HEADS, SW_KV_HEADS, SW_HEAD_DIM
    assert sub == Hk * hd and (Hq * hd) % tn == 0 and tn == 4 * sub
    n_q_tiles = Hq * hd // tn
    half = tn // 2
    kv_blk = Hq * hd // half
    r = lax.broadcasted_iota(jnp.int32, (sub, sub), 0) // hd
    c = lax.broadcasted_iota(jnp.int32, (sub, sub), 1) // hd
    blk = (r == c).astype(BF16)
    kern = functools.partial(_qkv_kernel, n_q_tiles=n_q_tiles, sub=sub)

    def first(i, j):
        return (0, jnp.minimum(2 * j, kv_blk))

    def second(i, j):
        return (0, jnp.minimum(2 * j + 1, kv_blk))

    return pl.pallas_call(
        kern,
        grid=(T // tm, n_q_tiles + 1),
        in_specs=[pl.BlockSpec((tm, D), lambda i, j: (i, 0)),
                  pl.BlockSpec((D, half), first),
                  pl.BlockSpec((D, half), second),
                  pl.BlockSpec((1, half), first),
                  pl.BlockSpec((1, half), second),
                  pl.BlockSpec((sub, sub), lambda i, j: (0, 0))],
        out_specs=pl.BlockSpec((tm, tn), lambda i, j: (i, j)),
        out_shape=jax.ShapeDtypeStruct((T, (n_q_tiles + 1) * tn), BF16),
        compiler_params=_params("arbitrary", "arbitrary"),
        name="swa_qkv_proj",
    )(h, w, w, gain_row, gain_row, blk)


def _swa_kernel(q_ref, kp_ref, kc_ref, vp_ref, vc_ref, sink_ref, o_ref):
    step = pl.program_id(1)
    BLK, hd = SW_BLOCK, SW_HEAD_DIM
    G = SW_HEADS // SW_KV_HEADS
    pairs = G // 2
    band = 2 * BLK
    nq = q_ref.shape[0] // BLK
    qi = lax.broadcasted_iota(jnp.int32, (BLK, band), 0)
    kr = lax.broadcasted_iota(jnp.int32, (BLK, band), 1)
    rel = BLK + qi - kr
    in_window = (rel >= 0) & (rel < BLK)
    bias_any = jnp.where(in_window, 0.0, -jnp.inf)
    bias_first = jnp.where(in_window & (kr >= jnp.where(step > 0, 0, BLK)), 0.0, -jnp.inf)
    lane = lax.broadcasted_iota(jnp.int32, (1, LANES), 1)
    lo = (lane < hd).astype(BF16)
    hi = (lane >= hd).astype(BF16)
    lo_f = lane < hd
    ones_lo = jnp.broadcast_to(lo, (band, LANES))
    ones_hi = jnp.broadcast_to(hi, (band, LANES))

    def band_rows(prev_ref, cur_ref, qb, cols):
        before = prev_ref[:, cols] if qb == 0 else cur_ref[(qb - 1) * BLK:qb * BLK, cols]
        return jnp.concatenate([before, cur_ref[qb * BLK:(qb + 1) * BLK, cols]], axis=0)

    units = [(qb, kh, p) for qb in range(nq) for kh in range(SW_KV_HEADS) for p in range(pairs)]

    scores = {}
    for qb in range(nq):
        for kh in range(SW_KV_HEADS):
            kd = band_rows(kp_ref, kc_ref, qb, slice(kh * LANES, (kh + 1) * LANES))
            kz = jnp.concatenate([kd * lo, kd * hi], axis=0)
            for p in range(pairs):
                q = q_ref[qb * BLK:(qb + 1) * BLK, (kh * pairs + p) * LANES:(kh * pairs + p + 1) * LANES]
                scores[qb, kh, p] = lax.dot_general(q, kz, (((1,), (1,)), ((), ())), preferred_element_type=F32)

    vz = None
    for qb, kh, p in units:
        if p == 0:
            vd = band_rows(vp_ref, vc_ref, qb, slice(kh * LANES, (kh + 1) * LANES))
            vz = jnp.concatenate([jnp.concatenate([vd * lo, ones_lo], axis=1),
                                  jnp.concatenate([vd * hi, ones_hi], axis=1)], axis=0)
        bias = bias_first if qb == 0 else bias_any
        s = scores[qb, kh, p]
        sink_b = sink_ref[kh, p * BLK:(p + 1) * BLK, :]
        es, mxs = [], []
        for half in range(2):
            sh = s[:, half * band:(half + 1) * band] + bias
            mx = jnp.maximum(jnp.max(sh, axis=1, keepdims=True), sink_b[:, half * hd:half * hd + 1])
            es.append(jnp.exp(sh - mx).astype(BF16))
            mxs.append(mx)
        o2 = _dot(jnp.concatenate(es, axis=1), vz)
        o = o2[:, :LANES] / (o2[:, LANES:] + jnp.exp(sink_b - jnp.where(lo_f, mxs[0], mxs[1])))
        o_ref[qb * BLK:(qb + 1) * BLK, (kh * pairs + p) * LANES:(kh * pairs + p + 1) * LANES] = o.astype(BF16)


def _swa(qkv, sink_cols, batch, seq, q_blocks=4):
    T = qkv.shape[0]
    BLK = SW_BLOCK
    rows = q_blocks * BLK
    ns = seq // rows
    qw = SW_HEADS * SW_HEAD_DIM
    kw = SW_KV_HEADS * LANES
    k_off = qw // kw
    v_off = k_off + 1

    def cur(b, j):
        return b * ns + j

    def prev(b, j):
        return (b * ns + j) * q_blocks - jnp.minimum(j, 1)

    return pl.pallas_call(
        _swa_kernel,
        grid=(batch, ns),
        in_specs=[pl.BlockSpec((rows, qw), lambda b, j: (cur(b, j), 0)),
                  pl.BlockSpec((BLK, kw), lambda b, j: (prev(b, j), k_off)),
                  pl.BlockSpec((rows, kw), lambda b, j: (cur(b, j), k_off)),
                  pl.BlockSpec((BLK, kw), lambda b, j: (prev(b, j), v_off)),
                  pl.BlockSpec((rows, kw), lambda b, j: (cur(b, j), v_off)),
                  pl.BlockSpec(sink_cols.shape, lambda b, j: (0, 0, 0))],
        out_specs=pl.BlockSpec((rows, qw), lambda b, j: (cur(b, j), 0)),
        out_shape=jax.ShapeDtypeStruct((T, qw), BF16),
        compiler_params=_params("arbitrary", "arbitrary"),
        name="swa_attention",
    )(qkv, qkv, qkv, qkv, qkv, sink_cols)


def kernel(x, c, l0_w_mod, l0_b_mod, l0_w_in, l0_gate_bias, l0_h_norm, l0_w_out, l0_w_up, l0_conv_w, l0_conv_b,
           l0_w_down, l1_w_mod, l1_b_mod, l1_w_qkv, l1_q_norm, l1_k_norm, l1_sinks, l1_w_out, l1_w_up, l1_conv_w,
           l1_conv_b, l1_w_down):
    B, S, D = x.shape
    T = B * S
    H, dk, dv = ML_HEADS, ML_DQK, ML_DV
    x2d = x.reshape(T, D)

    mod0 = _modulation(c, l0_w_mod, l0_b_mod).reshape(B, 6, D)
    mod1 = _modulation(c, l1_w_mod, l1_b_mod).reshape(B, 6, D)

    n_main = 2 * H * dk + 2 * H * dv
    w_main = l0_w_in.astype(BF16)
    w_gate = jnp.pad(l0_w_in[:, n_main:], ((0, 0), (0, LANES - 2 * H))).astype(BF16)
    gate_bias = jnp.pad(l0_gate_bias, (0, LANES - 2 * H)).reshape(1, LANES)
    col_scale = jnp.concatenate([jnp.ones((H * dk,), F32), jnp.full((H * dk,), dk ** -0.5, F32),
                                 jnp.ones((2 * H * dv,), F32)]).reshape(1, n_main)
    hm = _first_norm(x2d, mod0, S)
    qkvo, gcol, grow = _inproj(hm, w_main, col_scale, w_gate, gate_bias)
    y = _mlstm(qkvo, gcol, grow, l0_h_norm.reshape(1, H * dv), B, S)
    x1, hf = _outproj(y, l0_w_out, x2d, mod0, S)
    act = _ffn_up(hf, l0_w_up, l0_conv_w, l0_conv_b, S)
    x2, hm1 = _ffn_down(act, l0_w_down.astype(BF16), x1, mod0, mod1, S)

    Hq, Hk, hd = SW_HEADS, SW_KV_HEADS, SW_HEAD_DIM
    scale = hd ** -0.5

    gain = jnp.concatenate([jnp.tile(l1_q_norm * scale, Hq), jnp.tile(l1_k_norm, Hk),
                            jnp.ones((Hk * hd,), F32)]).reshape(1, -1)
    qkv = _qkv_proj(hm1, l1_w_qkv.astype(BF16), gain)
    pairs = Hq // Hk // 2
    sink_cols = jnp.repeat(jnp.repeat(l1_sinks.reshape(Hk, pairs, 2), SW_BLOCK, axis=1), hd, axis=2)
    att = _swa(qkv, sink_cols, B, S)
    x3, hf1 = _outproj(att, l1_w_out, x2, mod1, S)
    act1 = _ffn_up(hf1, l1_w_up, l1_conv_w, l1_conv_b, S)
    (x4,) = _ffn_down(act1, l1_w_down.astype(BF16), x3, mod1, None, S)
    return x4.reshape(B, S, D)
```

```python
import functools
import math

import jax
import jax.numpy as jnp
from jax import lax
from jax.experimental import pallas as pl
from jax.experimental.pallas import tpu as pltpu

EPS = 1e-6
ML_HEADS = 8
ML_DQK = 128
ML_DV = 256
GATE_CAP = 15.0
SW_HEADS = 32
SW_KV_HEADS = 4
SW_HEAD_DIM = 64
SW_BLOCK = 128
CONV_WIDTH = 3
MLSTM_CHUNK_UNROLL = 4
FFN_DOT_ROWS = 512
FFN_EPILOGUE_ROWS = 32

LANES = 128
SUBLANES = 8
VMEM_LIMIT = 56 * 1024 * 1024

F32 = jnp.float32
BF16 = jnp.bfloat16


def _params(*sem):
    return pltpu.CompilerParams(dimension_semantics=sem, vmem_limit_bytes=VMEM_LIMIT)


def _dot(a, b):
    return jnp.dot(a, b, preferred_element_type=F32)


def _rms_mod(x, scale_row, shift_row):
    ms = jnp.mean(x * x, axis=-1, keepdims=True)
    return (x * lax.rsqrt(ms + EPS)) * (1.0 + scale_row) + shift_row


def _mod_kernel(c_ref, w_ref, b_ref, o_ref):
    c = c_ref[...]
    sc = (c * jax.nn.sigmoid(c)).astype(BF16)
    o_ref[...] = _dot(sc, w_ref[...].astype(BF16)) + b_ref[...]


def _modulation(c, w_mod, b_mod, tn=1024):
    B, D = c.shape
    N = w_mod.shape[1]
    return pl.pallas_call(
        _mod_kernel,
        grid=(N // tn,),
        in_specs=[pl.BlockSpec((B, D), lambda j: (0, 0)),
                  pl.BlockSpec((D, tn), lambda j: (0, j)),
                  pl.BlockSpec((1, tn), lambda j: (0, j))],
        out_specs=pl.BlockSpec((B, tn), lambda j: (0, j)),
        out_shape=jax.ShapeDtypeStruct((B, N), F32),
        compiler_params=_params("arbitrary"),
        name="modulation",
    )(c, w_mod, b_mod.reshape(1, N))


def _norm_kernel(x_ref, mod_ref, o_ref):
    o_ref[...] = _rms_mod(x_ref[...], mod_ref[0, 1:2, :], mod_ref[0, 0:1, :]).astype(BF16)


def _first_norm(x2d, mod, seq, tm=1024):
    T, D = x2d.shape
    per = seq // tm
    return pl.pallas_call(
        _norm_kernel,
        grid=(T // tm,),
        in_specs=[pl.BlockSpec((tm, D), lambda i: (i, 0)),
                  pl.BlockSpec((1, 6, D), lambda i: (i // per, 0, 0))],
        out_specs=pl.BlockSpec((tm, D), lambda i: (i, 0)),
        out_shape=jax.ShapeDtypeStruct((T, D), BF16),
        compiler_params=_params("arbitrary"),
        name="first_norm",
    )(x2d, mod)


def _inproj_kernel(h_ref, w_ref, cs_ref, wg_ref, gb_ref, o_ref, gcol_ref, grow_ref):
    j = pl.program_id(1)
    h = h_ref[...]
    o_ref[...] = (_dot(h, w_ref[...]) * cs_ref[...]).astype(BF16)

    @pl.when(j == 0)
    def _():
        g = _dot(h, wg_ref[...]) + gb_ref[...]
        g = GATE_CAP * jnp.tanh(g * (1.0 / GATE_CAP))
        log_f = jnp.minimum(g, 0.0) - jnp.log(1.0 + jnp.exp(-jnp.abs(g)))
        lane = lax.broadcasted_iota(jnp.int32, g.shape, 1)
        g = jnp.where(lane < ML_HEADS, g, log_f)
        gcol_ref[...] = g
        grow_ref[...] = jnp.transpose(g)[: 2 * ML_HEADS, :]


def _inproj(h, w_main, col_scale, w_gate, gate_bias, tm=2048, tn=1024):
    T, D = h.shape
    N = col_scale.shape[1]
    return pl.pallas_call(
        _inproj_kernel,
        grid=(T // tm, N // tn),
        in_specs=[pl.BlockSpec((tm, D), lambda i, j: (i, 0)),
                  pl.BlockSpec((D, tn), lambda i, j: (0, j)),
                  pl.BlockSpec((1, tn), lambda i, j: (0, j)),
                  pl.BlockSpec((D, LANES), lambda i, j: (0, 0)),
                  pl.BlockSpec((1, LANES), lambda i, j: (0, 0))],
        out_specs=[pl.BlockSpec((tm, tn), lambda i, j: (i, j)),
                   pl.BlockSpec((tm, LANES), lambda i, j: (i, 0)),
                   pl.BlockSpec((2 * ML_HEADS, tm), lambda i, j: (0, i))],
        out_shape=[jax.ShapeDtypeStruct((T, N), BF16),
                   jax.ShapeDtypeStruct((T, LANES), F32),
                   jax.ShapeDtypeStruct((2 * ML_HEADS, T), F32)],
        compiler_params=_params("arbitrary", "arbitrary"),
        name="mlstm_inproj",
    )(h, w_main, col_scale, w_gate, gate_bias)


def _split_bf16(x):
    hi = x.astype(BF16)
    lo = (x - hi.astype(F32)).astype(BF16)
    return hi, lo


def _mlstm_kernel(q_ref, k_ref, v_ref, o_ref, gcol_ref, grow_ref, hn_ref, y_ref,
                  dexp_ref, wint_ref, floor_ref, wcol_ref, decay_ref, cn_ref,
                  *, heads_per_step, seq):
    L = LANES
    hg = heads_per_step
    nc = seq // L
    dk, dv = ML_DQK, ML_DV
    h0 = pl.program_id(1) * hg
    row = lax.broadcasted_iota(jnp.int32, (L, L), 0)
    col = lax.broadcasted_iota(jnp.int32, (L, L), 1)
    causal = col <= row
    tril = jnp.where(causal, 1.0, 0.0).astype(BF16)
    triu = jnp.where(row <= col, 1.0, 0.0).astype(BF16)

    sel_r = lax.broadcasted_iota(jnp.int32, (L, hg * L), 0)
    sel_head = h0 + lax.broadcasted_iota(jnp.int32, (L, hg * L), 1) // L
    sel_i = jnp.where(sel_r == sel_head, 1.0, 0.0).astype(BF16)
    sel_f = jnp.where(sel_r == sel_head + ML_HEADS, 1.0, 0.0).astype(BF16)
    gate_row = lax.broadcasted_iota(jnp.int32, (2 * ML_HEADS, 1), 0)
    m_run = [jnp.zeros((1, L), F32) for _ in range(hg)]
    spread = []
    for c in range(nc):
        rows = slice(c * L, (c + 1) * L)
        terms = _split_bf16(gcol_ref[rows, :])
        ib_all = sum(_dot(t, sel_i) for t in terms)
        f_terms = [_dot(t, sel_f).astype(BF16) for t in terms]
        grow = grow_ref[:, rows]
        cgrow = sum(_dot(t, triu) for t in _split_bf16(grow))
        spread.append((ib_all, f_terms, grow, cgrow))
    for c in range(nc):
        ib_all, f_terms, grow, cgrow = spread[c]
        bb_all = sum(_dot(tril, t) for t in f_terms)
        for hh in range(hg):
            head = h0 + hh
            m_c = m_run[hh]
            ib, bb = ib_all[:, hh * L:(hh + 1) * L], bb_all[:, hh * L:(hh + 1) * L]
            a_row = jnp.sum(jnp.where(gate_row == head, grow, 0.0)
                            - jnp.where(gate_row == head + ML_HEADS, cgrow, 0.0),
                            axis=0, keepdims=True)
            ca = jnp.max(jnp.where(causal, a_row, -jnp.inf), axis=1, keepdims=True)
            mm = jnp.maximum(m_c, jnp.broadcast_to(ca, (L, L)))
            total = bb[L - 1:L, :]
            m_next = total + mm[L - 1:L, :]
            dexp_ref[hh, c] = jnp.exp(jnp.where(causal, a_row - mm, -jnp.inf))
            wint_ref[hh, c] = jnp.exp(m_c - mm)
            floor_ref[hh, c] = jnp.exp(-(bb + mm))
            wcol_ref[hh, c] = jnp.exp(total + (ib - bb) - m_next)
            decay_ref[hh, c] = jnp.broadcast_to(jnp.exp(total + m_c - m_next), (SUBLANES, L))
            m_run[hh] = m_next

    cn_ref[...] = jnp.zeros_like(cn_ref)
    ones = jnp.ones((L, L), BF16)

    def chunk_step(ci, carry):
        r0 = pl.multiple_of(ci * L, L)
        early = []
        for hh in range(hg):
            q = q_ref[pl.ds(r0, L), hh * dk:(hh + 1) * dk]
            k = k_ref[pl.ds(r0, L), hh * dk:(hh + 1) * dk]
            v = v_ref[pl.ds(r0, L), hh * dv:(hh + 1) * dv]
            v_ext = jnp.concatenate([v, ones], axis=1)
            cn = cn_ref[hh]
            k_w = (k.astype(F32) * wcol_ref[hh, ci]).astype(BF16)
            kv = lax.dot_general(k_w, v_ext, (((0,), (0,)), ((), ())), preferred_element_type=F32)
            s = lax.dot_general(q, k, (((1,), (1,)), ((), ())), preferred_element_type=F32)
            inter = _dot(q, cn.astype(BF16))
            dec = decay_ref[hh, ci][0:1, :]
            cn_ref[hh] = jnp.concatenate([dec] * (cn.shape[1] // L), axis=1) * cn + kv
            early.append((s, inter, v_ext))
        for hh in range(hg):
            s, inter, v_ext = early[hh]
            p = (dexp_ref[hh, ci] * s).astype(BF16)
            intra = _dot(p, v_ext)
            wint = wint_ref[hh, ci]
            num = jnp.concatenate([wint, wint], axis=1) * inter[:, :dv] + intra[:, :dv]
            den = wint * inter[:, dv:] + intra[:, dv:]
            rdn = 1.0 / jnp.maximum(jnp.abs(den), floor_ref[hh, ci])
            hc = num * jnp.concatenate([rdn, rdn], axis=1)
            hs = hc * lax.rsqrt(jnp.mean(hc * hc, axis=1, keepdims=True) + EPS) * hn_ref[:, hh * dv:(hh + 1) * dv]
            og = o_ref[pl.ds(r0, L), hh * dv:(hh + 1) * dv].astype(F32)
            y_ref[pl.ds(r0, L), hh * dv:(hh + 1) * dv] = ((0.5 + 0.5 * jnp.tanh(0.5 * og)) * hs).astype(BF16)
        return carry

    lax.fori_loop(0, nc, chunk_step, 0, unroll=MLSTM_CHUNK_UNROLL)


def _mlstm(qkvo, gcol, grow, h_norm, batch, seq, heads_per_step=2):
    T = qkvo.shape[0]
    H, dk, dv = ML_HEADS, ML_DQK, ML_DV
    assert dk == LANES and seq % LANES == 0
    hg = heads_per_step
    nc = seq // LANES
    assert nc % MLSTM_CHUNK_UNROLL == 0
    qb, vb = hg * dk, hg * dv
    k_off = (H * dk) // qb
    v_off = (2 * H * dk) // vb
    o_off = (2 * H * dk + H * dv) // vb
    kern = functools.partial(_mlstm_kernel, heads_per_step=hg, seq=seq)
    factor = pltpu.VMEM((hg, nc, LANES, LANES), F32)
    return pl.pallas_call(
        kern,
        grid=(batch, H // hg),
        in_specs=[pl.BlockSpec((seq, qb), lambda b, g: (b, g)),
                  pl.BlockSpec((seq, qb), lambda b, g: (b, k_off + g)),
                  pl.BlockSpec((seq, vb), lambda b, g: (b, v_off + g)),
                  pl.BlockSpec((seq, vb), lambda b, g: (b, o_off + g)),
                  pl.BlockSpec((seq, LANES), lambda b, g: (b, 0)),
                  pl.BlockSpec((2 * H, seq), lambda b, g: (0, b)),
                  pl.BlockSpec((1, vb), lambda b, g: (0, g))],
        out_specs=pl.BlockSpec((seq, vb), lambda b, g: (b, g)),
        out_shape=jax.ShapeDtypeStruct((T, H * dv), BF16),
        scratch_shapes=[factor, factor, factor, factor,
                        pltpu.VMEM((hg, nc, SUBLANES, LANES), F32),
                        pltpu.VMEM((hg, dk, dv + LANES), F32)],
        compiler_params=_params("arbitrary", "arbitrary"),
        name="mlstm_chunks",
    )(qkvo, qkvo, qkvo, qkvo, gcol, grow, h_norm)


def _outproj_kernel(y_ref, w_ref, x_ref, mod_ref, xo_ref, ho_ref):
    acc = _dot(y_ref[...], w_ref[...].astype(BF16))
    x_new = x_ref[...] + mod_ref[0, 2:3, :] * acc
    xo_ref[...] = x_new
    ho_ref[...] = _rms_mod(x_new, mod_ref[0, 4:5, :], mod_ref[0, 3:4, :]).astype(BF16)


def _outproj(y, w, x2d, mod, seq, tm=512):
    T, K = y.shape
    D = w.shape[1]
    per = seq // tm
    return pl.pallas_call(
        _outproj_kernel,
        grid=(T // tm,),
        in_specs=[pl.BlockSpec((tm, K), lambda i: (i, 0)),
                  pl.BlockSpec((K, D), lambda i: (0, 0), pipeline_mode=pl.Buffered(1)),
                  pl.BlockSpec((tm, D), lambda i: (i, 0)),
                  pl.BlockSpec((1, 6, D), lambda i: (i // per, 0, 0))],
        out_specs=[pl.BlockSpec((tm, D), lambda i: (i, 0)),
                   pl.BlockSpec((tm, D), lambda i: (i, 0))],
        out_shape=[jax.ShapeDtypeStruct((T, D), F32),
                   jax.ShapeDtypeStruct((T, D), BF16)],
        compiler_params=_params("arbitrary"),
        name="outproj_residual_norm",
    )(y, w, x2d, mod)


def _ffn_up_kernel(h_ref, wg_ref, wv_ref, cwg_ref, cwv_ref, cbg_ref, cbv_ref, o_ref, ug_ref, uv_ref,
                   wbg_ref, wbv_ref, *, seq, sub):
    rb, rc = FFN_DOT_ROWS, FFN_EPILOGUE_ROWS
    units = [(s, b0) for s in range(o_ref.shape[1] // sub) for b0 in range(0, seq, rb)]

    def matmul(n, u_ref, w_ref, wb_ref):
        s, b0 = units[n]
        if b0 == 0:
            wb_ref[s % 2] = w_ref[:, s * sub:(s + 1) * sub].astype(BF16)
        u_ref[n % 2, :SUBLANES, :] = (u_ref[(n - 1) % 2, rb:, :] if b0 else jnp.zeros((SUBLANES, sub), F32))
        u_ref[n % 2, SUBLANES:, :] = _dot(h_ref[b0:b0 + rb, :], wb_ref[s % 2])

    def conv(u_ref, n, r0, cw, cb):
        out = cb + cw[2:3, :] * u_ref[n % 2, SUBLANES + r0:SUBLANES + r0 + rc, :]
        out = out + cw[1:2, :] * u_ref[n % 2, SUBLANES - 1 + r0:SUBLANES - 1 + r0 + rc, :]
        out = out + cw[0:1, :] * u_ref[n % 2, SUBLANES - 2 + r0:SUBLANES - 2 + r0 + rc, :]
        return out

    def epilogue(n, half):
        s, b0 = units[n]
        cols = slice(s * sub, (s + 1) * sub)
        cwg, cbg = 0.5 * cwg_ref[:, cols], 0.5 * cbg_ref[:, cols]
        cwv, cbv = cwv_ref[:, cols], cbv_ref[:, cols]
        for r0 in range(half * rb // 2, (half + 1) * rb // 2, rc):
            half_g = conv(ug_ref, n, r0, cwg, cbg)
            v = conv(uv_ref, n, r0, cwv, cbv)
            o_ref[b0 + r0:b0 + r0 + rc, cols] = ((half_g + half_g * jnp.tanh(half_g)) * v).astype(BF16)

    for n in range(len(units) + 1):
        for half, (u_ref, w_ref, wb_ref) in enumerate(((ug_ref, wg_ref, wbg_ref), (uv_ref, wv_ref, wbv_ref))):
            if n < len(units):
                matmul(n, u_ref, w_ref, wb_ref)
            if n:
                epilogue(n - 1, half)


def _ffn_up(h, w_up, conv_w, conv_b, seq, tn=512, sub=256):
    T, D = h.shape
    F = w_up.shape[1] // 2
    nf = F // tn
    kern = functools.partial(_ffn_up_kernel, seq=seq, sub=sub)
    cb = conv_b.reshape(1, 2 * F)
    return pl.pallas_call(
        kern,
        grid=(T // seq, nf),
        in_specs=[pl.BlockSpec((seq, D), lambda i, j: (i, 0)),
                  pl.BlockSpec((D, tn), lambda i, j: (0, j)),
                  pl.BlockSpec((D, tn), lambda i, j: (0, nf + j)),
                  pl.BlockSpec((CONV_WIDTH, tn), lambda i, j: (0, j)),
                  pl.BlockSpec((CONV_WIDTH, tn), lambda i, j: (0, nf + j)),
                  pl.BlockSpec((1, tn), lambda i, j: (0, j)),
                  pl.BlockSpec((1, tn), lambda i, j: (0, nf + j))],
        out_specs=pl.BlockSpec((seq, tn), lambda i, j: (i, j)),
        out_shape=jax.ShapeDtypeStruct((T, F), BF16),
        scratch_shapes=[pltpu.VMEM((2, SUBLANES + FFN_DOT_ROWS, sub), F32),
                        pltpu.VMEM((2, SUBLANES + FFN_DOT_ROWS, sub), F32),
                        pltpu.VMEM((2, D, sub), BF16), pltpu.VMEM((2, D, sub), BF16)],
        compiler_params=_params("arbitrary", "arbitrary"),
        name="ffn_up_conv_act",
    )(h, w_up, w_up, conv_w, conv_w, cb, cb)


def _ffn_down_kernel(a_ref, w_ref, x_ref, mod_ref, *rest, emit_norm):
    x_new = x_ref[...] + mod_ref[0, 5:6, :] * _dot(a_ref[...], w_ref[...])
    if emit_norm:
        nmod_ref, xo_ref, ho_ref = rest
        ho_ref[...] = _rms_mod(x_new, nmod_ref[0, 1:2, :], nmod_ref[0, 0:1, :]).astype(BF16)
    else:
        (xo_ref,) = rest
    xo_ref[...] = x_new


def _ffn_down(a, w, x2d, mod, next_mod, seq, tm=256):
    T, K = a.shape
    D = w.shape[1]
    per = seq // tm
    emit_norm = next_mod is not None
    mod_spec = pl.BlockSpec((1, 6, D), lambda i: (i // per, 0, 0))
    row_spec = pl.BlockSpec((tm, D), lambda i: (i, 0))
    in_specs = [pl.BlockSpec((tm, K), lambda i: (i, 0)),
                pl.BlockSpec((K, D), lambda i: (0, 0), pipeline_mode=pl.Buffered(1)),
                row_spec, mod_spec]
    args = [a, w, x2d, mod]
    out_specs = [row_spec]
    out_shape = [jax.ShapeDtypeStruct((T, D), F32)]
    if emit_norm:
        in_specs.append(mod_spec)
        args.append(next_mod)
        out_specs.append(row_spec)
        out_shape.append(jax.ShapeDtypeStruct((T, D), BF16))
    return pl.pallas_call(
        functools.partial(_ffn_down_kernel, emit_norm=emit_norm),
        grid=(T // tm,),
        in_specs=in_specs,
        out_specs=out_specs,
        out_shape=out_shape,
        compiler_params=_params("arbitrary"),
        name="ffn_down_residual_norm",
    )(*args)


def _qkv_kernel(h_ref, wa_ref, wb_ref, ga_ref, gb_ref, blk_ref, o_ref, *, n_q_tiles, sub):
    j = pl.program_id(1)
    h = h_ref[...]
    hd = SW_HEAD_DIM
    half = wa_ref.shape[1]

    def normed(acc, gain):
        ss = _dot((acc * acc).astype(BF16), blk_ref[...])
        return acc * lax.rsqrt(ss * (1.0 / hd) + EPS) * gain

    def duplicated(x):
        lo = lax.broadcasted_iota(jnp.int32, (1, LANES), 1) < hd
        out = []
        for c0 in range(0, x.shape[1], LANES):
            xc = x[:, c0:c0 + LANES]
            rolled = pltpu.roll(xc, hd, axis=1)
            out += [jnp.where(lo, xc, rolled), jnp.where(lo, rolled, xc)]
        return jnp.concatenate(out, axis=1)

    @pl.when(j < n_q_tiles)
    def _():
        srcs = [(w_ref, g_ref, slice(s, s + sub)) for w_ref, g_ref in ((wa_ref, ga_ref), (wb_ref, gb_ref))
                for s in range(0, half, sub)]
        accs = [_dot(h, w_ref[:, cols]) for w_ref, _, cols in srcs]
        for p, (_, g_ref, cols) in enumerate(srcs):
            o_ref[:, p * sub:(p + 1) * sub] = normed(accs[p], g_ref[:, cols]).astype(BF16)

    @pl.when(j == n_q_tiles)
    def _():
        k = _dot(h, wa_ref[:, :sub])
        v = _dot(h, wa_ref[:, sub:])
        o_ref[:, :half] = duplicated(normed(k, ga_ref[:, :sub])).astype(BF16)
        o_ref[:, half:] = duplicated(v).astype(BF16)


def _qkv_proj(h, w, gain_row, tm=2048, tn=1024, sub=256):
    T, D = h.shape
    Hq, Hk, hd = SW_HEADS, SW_KV_HEADS, SW_HEAD_DIM
    assert sub == Hk * hd and (Hq * hd) % tn == 0 and tn == 4 * sub
    n_q_tiles = Hq * hd // tn
    half = tn // 2
    kv_blk = Hq * hd // half
    r = lax.broadcasted_iota(jnp.int32, (sub, sub), 0) // hd
    c = lax.broadcasted_iota(jnp.int32, (sub, sub), 1) // hd
    blk = (r == c).astype(BF16)
    kern = functools.partial(_qkv_kernel, n_q_tiles=n_q_tiles, sub=sub)

    def first(i, j):
        return (0, jnp.minimum(2 * j, kv_blk))

    def second(i, j):
        return (0, jnp.minimum(2 * j + 1, kv_blk))

    return pl.pallas_call(
        kern,
        grid=(T // tm, n_q_tiles + 1),
        in_specs=[pl.BlockSpec((tm, D), lambda i, j: (i, 0)),
                  pl.BlockSpec((D, half), first),
                  pl.BlockSpec((D, half), second),
                  pl.BlockSpec((1, half), first),
                  pl.BlockSpec((1, half), second),
                  pl.BlockSpec((sub, sub), lambda i, j: (0, 0))],
        out_specs=pl.BlockSpec((tm, tn), lambda i, j: (i, j)),
        out_shape=jax.ShapeDtypeStruct((T, (n_q_tiles + 1) * tn), BF16),
        compiler_params=_params("arbitrary", "arbitrary"),
        name="swa_qkv_proj",
    )(h, w, w, gain_row, gain_row, blk)


def _swa_kernel(q_ref, kp_ref, kc_ref, vp_ref, vc_ref, sink_ref, o_ref):
    step = pl.program_id(1)
    BLK, hd = SW_BLOCK, SW_HEAD_DIM
    G = SW_HEADS // SW_KV_HEADS
    pairs = G // 2
    band = 2 * BLK
    nq = q_ref.shape[0] // BLK
    qi = lax.broadcasted_iota(jnp.int32, (BLK, band), 0)
    kr = lax.broadcasted_iota(jnp.int32, (BLK, band), 1)
    rel = BLK + qi - kr
    in_window = (rel >= 0) & (rel < BLK)
    bias_any = jnp.where(in_window, 0.0, -jnp.inf)
    bias_first = jnp.where(in_window & (kr >= jnp.where(step > 0, 0, BLK)), 0.0, -jnp.inf)
    lane = lax.broadcasted_iota(jnp.int32, (1, LANES), 1)
    lo = (lane < hd).astype(BF16)
    hi = (lane >= hd).astype(BF16)
    lo_f = lane < hd
    ones_lo = jnp.broadcast_to(lo, (band, LANES))
    ones_hi = jnp.broadcast_to(hi, (band, LANES))

    def band_rows(prev_ref, cur_ref, qb, cols):
        before = prev_ref[:, cols] if qb == 0 else cur_ref[(qb - 1) * BLK:qb * BLK, cols]
        return jnp.concatenate([before, cur_ref[qb * BLK:(qb + 1) * BLK, cols]], axis=0)

    units = [(qb, kh, p) for qb in range(nq) for kh in range(SW_KV_HEADS) for p in range(pairs)]

    scores = {}
    for qb in range(nq):
        for kh in range(SW_KV_HEADS):
            kd = band_rows(kp_ref, kc_ref, qb, slice(kh * LANES, (kh + 1) * LANES))
            kz = jnp.concatenate([kd * lo, kd * hi], axis=0)
            for p in range(pairs):
                q = q_ref[qb * BLK:(qb + 1) * BLK, (kh * pairs + p) * LANES:(kh * pairs + p + 1) * LANES]
                scores[qb, kh, p] = lax.dot_general(q, kz, (((1,), (1,)), ((), ())), preferred_element_type=F32)

    vz = None
    for qb, kh, p in units:
        if p == 0:
            vd = band_rows(vp_ref, vc_ref, qb, slice(kh * LANES, (kh + 1) * LANES))
            vz = jnp.concatenate([jnp.concatenate([vd * lo, ones_lo], axis=1),
                                  jnp.concatenate([vd * hi, ones_hi], axis=1)], axis=0)
        bias = bias_first if qb == 0 else bias_any
        s = scores[qb, kh, p]
        sink_b = sink_ref[kh, p * BLK:(p + 1) * BLK, :]
        es, mxs = [], []
        for half in range(2):
            sh = s[:, half * band:(half + 1) * band] + bias
            mx = jnp.maximum(jnp.max(sh, axis=1, keepdims=True), sink_b[:, half * hd:half * hd + 1])
            es.append(jnp.exp(sh - mx).astype(BF16))
            mxs.append(mx)
        o2 = _dot(jnp.concatenate(es, axis=1), vz)
        o = o2[:, :LANES] / (o2[:, LANES:] + jnp.exp(sink_b - jnp.where(lo_f, mxs[0], mxs[1])))
        o_ref[qb * BLK:(qb + 1) * BLK, (kh * pairs + p) * LANES:(kh * pairs + p + 1) * LANES] = o.astype(BF16)


def _swa(qkv, sink_cols, batch, seq, q_blocks=4):
    T = qkv.shape[0]
    BLK = SW_BLOCK
    rows = q_blocks * BLK
    ns = seq // rows
    qw = SW_HEADS * SW_HEAD_DIM
    kw = SW_KV_HEADS * LANES
    k_off = qw // kw
    v_off = k_off + 1

    def cur(b, j):
        return b * ns + j

    def prev(b, j):
        return (b * ns + j) * q_blocks - jnp.minimum(j, 1)

    return pl.pallas_call(
        _swa_kernel,
        grid=(batch, ns),
        in_specs=[pl.BlockSpec((rows, qw), lambda b, j: (cur(b, j), 0)),
                  pl.BlockSpec((BLK, kw), lambda b, j: (prev(b, j), k_off)),
                  pl.BlockSpec((rows, kw), lambda b, j: (cur(b, j), k_off)),
                  pl.BlockSpec((BLK, kw), lambda b, j: (prev(b, j), v_off)),
                  pl.BlockSpec((rows, kw), lambda b, j: (cur(b, j), v_off)),
                  pl.BlockSpec(sink_cols.shape, lambda b, j: (0, 0, 0))],
        out_specs=pl.BlockSpec((rows, qw), lambda b, j: (cur(b, j), 0)),
        out_shape=jax.ShapeDtypeStruct((T, qw), BF16),
        compiler_params=_params("arbitrary", "arbitrary"),
        name="swa_attention",
    )(qkv, qkv, qkv, qkv, qkv, sink_cols)


def kernel(x, c, l0_w_mod, l0_b_mod, l0_w_in, l0_gate_bias, l0_h_norm, l0_w_out, l0_w_up, l0_conv_w, l0_conv_b,
           l0_w_down, l1_w_mod, l1_b_mod, l1_w_qkv, l1_q_norm, l1_k_norm, l1_sinks, l1_w_out, l1_w_up, l1_conv_w,
           l1_conv_b, l1_w_down):
    B, S, D = x.shape
    T = B * S
    H, dk, dv = ML_HEADS, ML_DQK, ML_DV
    x2d = x.reshape(T, D)

    mod0 = _modulation(c, l0_w_mod, l0_b_mod).reshape(B, 6, D)
    mod1 = _modulation(c, l1_w_mod, l1_b_mod).reshape(B, 6, D)

    n_main = 2 * H * dk + 2 * H * dv
    w_main = l0_w_in.astype(BF16)
    w_gate = jnp.pad(l0_w_in[:, n_main:], ((0, 0), (0, LANES - 2 * H))).astype(BF16)
    gate_bias = jnp.pad(l0_gate_bias, (0, LANES - 2 * H)).reshape(1, LANES)
    col_scale = jnp.concatenate([jnp.ones((H * dk,), F32), jnp.full((H * dk,), dk ** -0.5, F32),
                                 jnp.ones((2 * H * dv,), F32)]).reshape(1, n_main)
    hm = _first_norm(x2d, mod0, S)
    qkvo, gcol, grow = _inproj(hm, w_main, col_scale, w_gate, gate_bias)
    y = _mlstm(qkvo, gcol, grow, l0_h_norm.reshape(1, H * dv), B, S)
    x1, hf = _outproj(y, l0_w_out, x2d, mod0, S)
    act = _ffn_up(hf, l0_w_up, l0_conv_w, l0_conv_b, S)
    x2, hm1 = _ffn_down(act, l0_w_down.astype(BF16), x1, mod0, mod1, S)

    Hq, Hk, hd = SW_HEADS, SW_KV_HEADS, SW_HEAD_DIM
    scale = hd ** -0.5

    gain = jnp.concatenate([jnp.tile(l1_q_norm * scale, Hq), jnp.tile(l1_k_norm, Hk),
                            jnp.ones((Hk * hd,), F32)]).reshape(1, -1)
    qkv = _qkv_proj(hm1, l1_w_qkv.astype(BF16), gain)
    pairs = Hq // Hk // 2
    sink_cols = jnp.repeat(jnp.repeat(l1_sinks.reshape(Hk, pairs, 2), SW_BLOCK, axis=1), hd, axis=2)
    att = _swa(qkv, sink_cols, B, S)
    x3, hf1 = _outproj(att, l1_w_out, x2, mod1, S)
    act1 = _ffn_up(hf1, l1_w_up, l1_conv_w, l1_conv_b, S)
    (x4,) = _ffn_down(act1, l1_w_down.astype(BF16), x3, mod1, None, S)
    return x4.reshape(B, S, D)
```

```python
import functools
import math

import jax
import jax.numpy as jnp
from jax import lax
from jax.experimental import pallas as pl
from jax.experimental.pallas import tpu as pltpu

EPS = 1e-6
ML_HEADS = 8
ML_DQK = 128
ML_DV = 256
GATE_CAP = 15.0
SW_HEADS = 32
SW_KV_HEADS = 4
SW_HEAD_DIM = 64
SW_BLOCK = 128
CONV_WIDTH = 3
MLSTM_CHUNK_UNROLL = 8
FFN_DOT_ROWS = 512
FFN_EPILOGUE_ROWS = 32

LANES = 128
SUBLANES = 8
VMEM_LIMIT = 56 * 1024 * 1024

F32 = jnp.float32
BF16 = jnp.bfloat16


def _params(*sem):
    return pltpu.CompilerParams(dimension_semantics=sem, vmem_limit_bytes=VMEM_LIMIT)


def _dot(a, b):
    return jnp.dot(a, b, preferred_element_type=F32)


def _rms_mod(x, scale_row, shift_row):
    ms = jnp.mean(x * x, axis=-1, keepdims=True)
    return (x * lax.rsqrt(ms + EPS)) * (1.0 + scale_row) + shift_row


def _mod_kernel(c_ref, w_ref, b_ref, o_ref):
    c = c_ref[...]
    sc = (c * jax.nn.sigmoid(c)).astype(BF16)
    o_ref[...] = _dot(sc, w_ref[...].astype(BF16)) + b_ref[...]


def _modulation(c, w_mod, b_mod, tn=1024):
    B, D = c.shape
    N = w_mod.shape[1]
    return pl.pallas_call(
        _mod_kernel,
        grid=(N // tn,),
        in_specs=[pl.BlockSpec((B, D), lambda j: (0, 0)),
                  pl.BlockSpec((D, tn), lambda j: (0, j)),
                  pl.BlockSpec((1, tn), lambda j: (0, j))],
        out_specs=pl.BlockSpec((B, tn), lambda j: (0, j)),
        out_shape=jax.ShapeDtypeStruct((B, N), F32),
        compiler_params=_params("arbitrary"),
        name="modulation",
    )(c, w_mod, b_mod.reshape(1, N))


def _norm_kernel(x_ref, mod_ref, o_ref):
    o_ref[...] = _rms_mod(x_ref[...], mod_ref[0, 1:2, :], mod_ref[0, 0:1, :]).astype(BF16)


def _first_norm(x2d, mod, seq, tm=1024):
    T, D = x2d.shape
    per = seq // tm
    return pl.pallas_call(
        _norm_kernel,
        grid=(T // tm,),
        in_specs=[pl.BlockSpec((tm, D), lambda i: (i, 0)),
                  pl.BlockSpec((1, 6, D), lambda i: (i // per, 0, 0))],
        out_specs=pl.BlockSpec((tm, D), lambda i: (i, 0)),
        out_shape=jax.ShapeDtypeStruct((T, D), BF16),
        compiler_params=_params("arbitrary"),
        name="first_norm",
    )(x2d, mod)


def _inproj_kernel(h_ref, w_ref, cs_ref, wg_ref, gb_ref, o_ref, gcol_ref, grow_ref):
    j = pl.program_id(1)
    h = h_ref[...]
    o_ref[...] = (_dot(h, w_ref[...]) * cs_ref[...]).astype(BF16)

    @pl.when(j == 0)
    def _():
        g = _dot(h, wg_ref[...]) + gb_ref[...]
        g = GATE_CAP * jnp.tanh(g * (1.0 / GATE_CAP))
        log_f = jnp.minimum(g, 0.0) - jnp.log(1.0 + jnp.exp(-jnp.abs(g)))
        lane = lax.broadcasted_iota(jnp.int32, g.shape, 1)
        g = jnp.where(lane < ML_HEADS, g, log_f)
        gcol_ref[...] = g
        grow_ref[...] = jnp.transpose(g)[: 2 * ML_HEADS, :]


def _inproj(h, w_main, col_scale, w_gate, gate_bias, tm=2048, tn=1024):
    T, D = h.shape
    N = col_scale.shape[1]
    return pl.pallas_call(
        _inproj_kernel,
        grid=(T // tm, N // tn),
        in_specs=[pl.BlockSpec((tm, D), lambda i, j: (i, 0)),
                  pl.BlockSpec((D, tn), lambda i, j: (0, j)),
                  pl.BlockSpec((1, tn), lambda i, j: (0, j)),
                  pl.BlockSpec((D, LANES), lambda i, j: (0, 0)),
                  pl.BlockSpec((1, LANES), lambda i, j: (0, 0))],
        out_specs=[pl.BlockSpec((tm, tn), lambda i, j: (i, j)),
                   pl.BlockSpec((tm, LANES), lambda i, j: (i, 0)),
                   pl.BlockSpec((2 * ML_HEADS, tm), lambda i, j: (0, i))],
        out_shape=[jax.ShapeDtypeStruct((T, N), BF16),
                   jax.ShapeDtypeStruct((T, LANES), F32),
                   jax.ShapeDtypeStruct((2 * ML_HEADS, T), F32)],
        compiler_params=_params("arbitrary", "arbitrary"),
        name="mlstm_inproj",
    )(h, w_main, col_scale, w_gate, gate_bias)


def _split_bf16(x):
    hi = x.astype(BF16)
    lo = (x - hi.astype(F32)).astype(BF16)
    return hi, lo


def _mlstm_kernel(q_ref, k_ref, v_ref, o_ref, gcol_ref, grow_ref, hn_ref, y_ref,
                  dexp_ref, wint_ref, floor_ref, wcol_ref, decay_ref, cn_ref,
                  *, heads_per_step, seq):
    L = LANES
    hg = heads_per_step
    nc = seq // L
    dk, dv = ML_DQK, ML_DV
    h0 = pl.program_id(1) * hg
    row = lax.broadcasted_iota(jnp.int32, (L, L), 0)
    col = lax.broadcasted_iota(jnp.int32, (L, L), 1)
    causal = col <= row
    tril = jnp.where(causal, 1.0, 0.0).astype(BF16)
    triu = jnp.where(row <= col, 1.0, 0.0).astype(BF16)

    sel_r = lax.broadcasted_iota(jnp.int32, (L, hg * L), 0)
    sel_head = h0 + lax.broadcasted_iota(jnp.int32, (L, hg * L), 1) // L
    sel_i = jnp.where(sel_r == sel_head, 1.0, 0.0).astype(BF16)
    sel_f = jnp.where(sel_r == sel_head + ML_HEADS, 1.0, 0.0).astype(BF16)
    gate_row = lax.broadcasted_iota(jnp.int32, (2 * ML_HEADS, 1), 0)
    m_run = [jnp.zeros((1, L), F32) for _ in range(hg)]
    spread = []
    for c in range(nc):
        rows = slice(c * L, (c + 1) * L)
        terms = _split_bf16(gcol_ref[rows, :])
        ib_all = sum(_dot(t, sel_i) for t in terms)
        f_terms = [_dot(t, sel_f).astype(BF16) for t in terms]
        grow = grow_ref[:, rows]
        cgrow = sum(_dot(t, triu) for t in _split_bf16(grow))
        spread.append((ib_all, f_terms, grow, cgrow))
    for c in range(nc):
        ib_all, f_terms, grow, cgrow = spread[c]
        bb_all = sum(_dot(tril, t) for t in f_terms)
        for hh in range(hg):
            head = h0 + hh
            m_c = m_run[hh]
            ib, bb = ib_all[:, hh * L:(hh + 1) * L], bb_all[:, hh * L:(hh + 1) * L]
            a_row = jnp.sum(jnp.where(gate_row == head, grow, 0.0)
                            - jnp.where(gate_row == head + ML_HEADS, cgrow, 0.0),
                            axis=0, keepdims=True)
            ca = jnp.max(jnp.where(causal, a_row, -jnp.inf), axis=1, keepdims=True)
            mm = jnp.maximum(m_c, jnp.broadcast_to(ca, (L, L)))
            total = bb[L - 1:L, :]
            m_next = total + mm[L - 1:L, :]
            dexp_ref[hh, c] = jnp.exp(jnp.where(causal, a_row - mm, -jnp.inf))
            wint_ref[hh, c] = jnp.exp(m_c - mm)
            floor_ref[hh, c] = jnp.exp(-(bb + mm))
            wcol_ref[hh, c] = jnp.exp(total + (ib - bb) - m_next)
            decay_ref[hh, c] = jnp.broadcast_to(jnp.exp(total + m_c - m_next), (SUBLANES, L))
            m_run[hh] = m_next

    cn_ref[...] = jnp.zeros_like(cn_ref)
    ones = jnp.ones((L, L), BF16)

    def chunk_step(ci, carry):
        r0 = pl.multiple_of(ci * L, L)
        early = []
        for hh in range(hg):
            q = q_ref[pl.ds(r0, L), hh * dk:(hh + 1) * dk]
            k = k_ref[pl.ds(r0, L), hh * dk:(hh + 1) * dk]
            v = v_ref[pl.ds(r0, L), hh * dv:(hh + 1) * dv]
            v_ext = jnp.concatenate([v, ones], axis=1)
            cn = cn_ref[hh]
            k_w = (k.astype(F32) * wcol_ref[hh, ci]).astype(BF16)
            kv = lax.dot_general(k_w, v_ext, (((0,), (0,)), ((), ())), preferred_element_type=F32)
            s = lax.dot_general(q, k, (((1,), (1,)), ((), ())), preferred_element_type=F32)
            inter = _dot(q, cn.astype(BF16))
            dec = decay_ref[hh, ci][0:1, :]
            cn_ref[hh] = jnp.concatenate([dec] * (cn.shape[1] // L), axis=1) * cn + kv
            early.append((s, inter, v_ext))
        for hh in range(hg):
            s, inter, v_ext = early[hh]
            p = (dexp_ref[hh, ci] * s).astype(BF16)
            intra = _dot(p, v_ext)
            wint = wint_ref[hh, ci]
            num = jnp.concatenate([wint, wint], axis=1) * inter[:, :dv] + intra[:, :dv]
            den = wint * inter[:, dv:] + intra[:, dv:]
            rdn = 1.0 / jnp.maximum(jnp.abs(den), floor_ref[hh, ci])
            hc = num * jnp.concatenate([rdn, rdn], axis=1)
            half_gain = 0.5 * hn_ref[:, hh * dv:(hh + 1) * dv]
            hs = hc * lax.rsqrt(jnp.mean(hc * hc, axis=1, keepdims=True) + EPS) * half_gain
            og = o_ref[pl.ds(r0, L), hh * dv:(hh + 1) * dv].astype(F32)
            y_ref[pl.ds(r0, L), hh * dv:(hh + 1) * dv] = ((1.0 + jnp.tanh(0.5 * og)) * hs).astype(BF16)
        return carry

    lax.fori_loop(0, nc, chunk_step, 0, unroll=MLSTM_CHUNK_UNROLL)


def _mlstm(qkvo, gcol, grow, h_norm, batch, seq, heads_per_step=2):
    T = qkvo.shape[0]
    H, dk, dv = ML_HEADS, ML_DQK, ML_DV
    assert dk == LANES and seq % LANES == 0
    hg = heads_per_step
    nc = seq // LANES
    assert nc % MLSTM_CHUNK_UNROLL == 0
    qb, vb = hg * dk, hg * dv
    k_off = (H * dk) // qb
    v_off = (2 * H * dk) // vb
    o_off = (2 * H * dk + H * dv) // vb
    kern = functools.partial(_mlstm_kernel, heads_per_step=hg, seq=seq)
    factor = pltpu.VMEM((hg, nc, LANES, LANES), F32)
    return pl.pallas_call(
        kern,
        grid=(batch, H // hg),
        in_specs=[pl.BlockSpec((seq, qb), lambda b, g: (b, g)),
                  pl.BlockSpec((seq, qb), lambda b, g: (b, k_off + g)),
                  pl.BlockSpec((seq, vb), lambda b, g: (b, v_off + g)),
                  pl.BlockSpec((seq, vb), lambda b, g: (b, o_off + g)),
                  pl.BlockSpec((seq, LANES), lambda b, g: (b, 0)),
                  pl.BlockSpec((2 * H, seq), lambda b, g: (0, b)),
                  pl.BlockSpec((1, vb), lambda b, g: (0, g))],
        out_specs=pl.BlockSpec((seq, vb), lambda b, g: (b, g)),
        out_shape=jax.ShapeDtypeStruct((T, H * dv), BF16),
        scratch_shapes=[factor, factor, factor, factor,
                        pltpu.VMEM((hg, nc, SUBLANES, LANES), F32),
                        pltpu.VMEM((hg, dk, dv + LANES), F32)],
        compiler_params=_params("arbitrary", "arbitrary"),
        name="mlstm_chunks",
    )(qkvo, qkvo, qkvo, qkvo, gcol, grow, h_norm)


def _outproj_kernel(y_ref, w_ref, x_ref, mod_ref, xo_ref, ho_ref):
    acc = _dot(y_ref[...], w_ref[...].astype(BF16))
    x_new = x_ref[...] + mod_ref[0, 2:3, :] * acc
    xo_ref[...] = x_new
    ho_ref[...] = _rms_mod(x_new, mod_ref[0, 4:5, :], mod_ref[0, 3:4, :]).astype(BF16)


def _outproj(y, w, x2d, mod, seq, tm=512):
    T, K = y.shape
    D = w.shape[1]
    per = seq // tm
    return pl.pallas_call(
        _outproj_kernel,
        grid=(T // tm,),
        in_specs=[pl.BlockSpec((tm, K), lambda i: (i, 0)),
                  pl.BlockSpec((K, D), lambda i: (0, 0), pipeline_mode=pl.Buffered(1)),
                  pl.BlockSpec((tm, D), lambda i: (i, 0)),
                  pl.BlockSpec((1, 6, D), lambda i: (i // per, 0, 0))],
        out_specs=[pl.BlockSpec((tm, D), lambda i: (i, 0)),
                   pl.BlockSpec((tm, D), lambda i: (i, 0))],
        out_shape=[jax.ShapeDtypeStruct((T, D), F32),
                   jax.ShapeDtypeStruct((T, D), BF16)],
        compiler_params=_params("arbitrary"),
        name="outproj_residual_norm",
    )(y, w, x2d, mod)


def _ffn_up_kernel(h_ref, wg_ref, wv_ref, cwg_ref, cwv_ref, cbg_ref, cbv_ref, o_ref, ug_ref, uv_ref,
                   wbg_ref, wbv_ref, *, seq, sub):
    rb, rc = FFN_DOT_ROWS, FFN_EPILOGUE_ROWS
    units = [(s, b0) for s in range(o_ref.shape[1] // sub) for b0 in range(0, seq, rb)]

    def matmul(n, u_ref, w_ref, wb_ref):
        s, b0 = units[n]
        if b0 == 0:
            wb_ref[s % 2] = w_ref[:, s * sub:(s + 1) * sub].astype(BF16)
        u_ref[n % 2, :SUBLANES, :] = (u_ref[(n - 1) % 2, rb:, :] if b0 else jnp.zeros((SUBLANES, sub), F32))
        u_ref[n % 2, SUBLANES:, :] = _dot(h_ref[b0:b0 + rb, :], wb_ref[s % 2])

    def conv(u_ref, n, r0, cw, cb):
        out = cb + cw[2:3, :] * u_ref[n % 2, SUBLANES + r0:SUBLANES + r0 + rc, :]
        out = out + cw[1:2, :] * u_ref[n % 2, SUBLANES - 1 + r0:SUBLANES - 1 + r0 + rc, :]
        out = out + cw[0:1, :] * u_ref[n % 2, SUBLANES - 2 + r0:SUBLANES - 2 + r0 + rc, :]
        return out

    def epilogue(n, half):
        s, b0 = units[n]
        cols = slice(s * sub, (s + 1) * sub)
        cwg, cbg = 0.5 * cwg_ref[:, cols], 0.5 * cbg_ref[:, cols]
        cwv, cbv = cwv_ref[:, cols], cbv_ref[:, cols]
        for r0 in range(half * rb // 2, (half + 1) * rb // 2, rc):
            half_g = conv(ug_ref, n, r0, cwg, cbg)
            v = conv(uv_ref, n, r0, cwv, cbv)
            o_ref[b0 + r0:b0 + r0 + rc, cols] = ((half_g + half_g * jnp.tanh(half_g)) * v).astype(BF16)

    for n in range(len(units) + 1):
        for half, (u_ref, w_ref, wb_ref) in enumerate(((ug_ref, wg_ref, wbg_ref), (uv_ref, wv_ref, wbv_ref))):
            if n < len(units):
                matmul(n, u_ref, w_ref, wb_ref)
            if n:
                epilogue(n - 1, half)


def _ffn_up(h, w_up, conv_w, conv_b, seq, tn=512, sub=256):
    T, D = h.shape
    F = w_up.shape[1] // 2
    nf = F // tn
    kern = functools.partial(_ffn_up_kernel, seq=seq, sub=sub)
    cb = conv_b.reshape(1, 2 * F)
    return pl.pallas_call(
        kern,
        grid=(T // seq, nf),
        in_specs=[pl.BlockSpec((seq, D), lambda i, j: (i, 0)),
                  pl.BlockSpec((D, tn), lambda i, j: (0, j)),
                  pl.BlockSpec((D, tn), lambda i, j: (0, nf + j)),
                  pl.BlockSpec((CONV_WIDTH, tn), lambda i, j: (0, j)),
                  pl.BlockSpec((CONV_WIDTH, tn), lambda i, j: (0, nf + j)),
                  pl.BlockSpec((1, tn), lambda i, j: (0, j)),
                  pl.BlockSpec((1, tn), lambda i, j: (0, nf + j))],
        out_specs=pl.BlockSpec((seq, tn), lambda i, j: (i, j)),
        out_shape=jax.ShapeDtypeStruct((T, F), BF16),
        scratch_shapes=[pltpu.VMEM((2, SUBLANES + FFN_DOT_ROWS, sub), F32),
                        pltpu.VMEM((2, SUBLANES + FFN_DOT_ROWS, sub), F32),
                        pltpu.VMEM((2, D, sub), BF16), pltpu.VMEM((2, D, sub), BF16)],
        compiler_params=_params("arbitrary", "arbitrary"),
        name="ffn_up_conv_act",
    )(h, w_up, w_up, conv_w, conv_w, cb, cb)


def _ffn_down_kernel(a_ref, w_ref, x_ref, mod_ref, *rest, emit_norm):
    x_new = x_ref[...] + mod_ref[0, 5:6, :] * _dot(a_ref[...], w_ref[...])
    if emit_norm:
        nmod_ref, xo_ref, ho_ref = rest
        ho_ref[...] = _rms_mod(x_new, nmod_ref[0, 1:2, :], nmod_ref[0, 0:1, :]).astype(BF16)
    else:
        (xo_ref,) = rest
    xo_ref[...] = x_new


def _ffn_down(a, w, x2d, mod, next_mod, seq, tm=256):
    T, K = a.shape
    D = w.shape[1]
    per = seq // tm
    emit_norm = next_mod is not None
    mod_spec = pl.BlockSpec((1, 6, D), lambda i: (i // per, 0, 0))
    row_spec = pl.BlockSpec((tm, D), lambda i: (i, 0))
    in_specs = [pl.BlockSpec((tm, K), lambda i: (i, 0)),
                pl.BlockSpec((K, D), lambda i: (0, 0), pipeline_mode=pl.Buffered(1)),
                row_spec, mod_spec]
    args = [a, w, x2d, mod]
    out_specs = [row_spec]
    out_shape = [jax.ShapeDtypeStruct((T, D), F32)]
    if emit_norm:
        in_specs.append(mod_spec)
        args.append(next_mod)
        out_specs.append(row_spec)
        out_shape.append(jax.ShapeDtypeStruct((T, D), BF16))
    return pl.pallas_call(
        functools.partial(_ffn_down_kernel, emit_norm=emit_norm),
        grid=(T // tm,),
        in_specs=in_specs,
        out_specs=out_specs,
        out_shape=out_shape,
        compiler_params=_params("arbitrary"),
        name="ffn_down_residual_norm",
    )(*args)


def _qkv_kernel(h_ref, wa_ref, wb_ref, ga_ref, gb_ref, blk_ref, o_ref, *, n_q_tiles, sub):
    j = pl.program_id(1)
    h = h_ref[...]
    hd = SW_HEAD_DIM
    half = wa_ref.shape[1]

    def normed(acc, gain):
        ss = _dot((acc * acc).astype(BF16), blk_ref[...])
        return acc * lax.rsqrt(ss * (1.0 / hd) + EPS) * gain

    def duplicated(x):
        lo = lax.broadcasted_iota(jnp.int32, (1, LANES), 1) < hd
        out = []
        for c0 in range(0, x.shape[1], LANES):
            xc = x[:, c0:c0 + LANES]
            rolled = pltpu.roll(xc, hd, axis=1)
            out += [jnp.where(lo, xc, rolled), jnp.where(lo, rolled, xc)]
        return jnp.concatenate(out, axis=1)

    @pl.when(j < n_q_tiles)
    def _():
        srcs = [(w_ref, g_ref, slice(s, s + sub)) for w_ref, g_ref in ((wa_ref, ga_ref), (wb_ref, gb_ref))
                for s in range(0, half, sub)]
        accs = [_dot(h, w_ref[:, cols]) for w_ref, _, cols in srcs]
        for p, (_, g_ref, cols) in enumerate(srcs):
            o_ref[:, p * sub:(p + 1) * sub] = normed(accs[p], g_ref[:, cols]).astype(BF16)

    @pl.when(j == n_q_tiles)
    def _():
        k = _dot(h, wa_ref[:, :sub])
        v = _dot(h, wa_ref[:, sub:])
        o_ref[:, :half] = duplicated(normed(k, ga_ref[:, :sub])).astype(BF16)
        o_ref[:, half:] = duplicated(v).astype(BF16)


def _qkv_proj(h, w, gain_row, tm=2048, tn=1024, sub=256):
    T, D = h.shape
    Hq, Hk, hd = SW_HEADS, SW_KV_HEADS, SW_HEAD_DIM
    assert sub == Hk * hd and (Hq * hd) % tn == 0 and tn == 4 * sub
    n_q_tiles = Hq * hd // tn
    half = tn // 2
    kv_blk = Hq * hd // half
    r = lax.broadcasted_iota(jnp.int32, (sub, sub), 0) // hd
    c = lax.broadcasted_iota(jnp.int32, (sub, sub), 1) // hd
    blk = (r == c).astype(BF16)
    kern = functools.partial(_qkv_kernel, n_q_tiles=n_q_tiles, sub=sub)

    def first(i, j):
        return (0, jnp.minimum(2 * j, kv_blk))

    def second(i, j):
        return (0, jnp.minimum(2 * j + 1, kv_blk))

    return pl.pallas_call(
        kern,
        grid=(T // tm, n_q_tiles + 1),
        in_specs=[pl.BlockSpec((tm, D), lambda i, j: (i, 0)),
                  pl.BlockSpec((D, half), first),
                  pl.BlockSpec((D, half), second),
                  pl.BlockSpec((1, half), first),
                  pl.BlockSpec((1, half), second),
                  pl.BlockSpec((sub, sub), lambda i, j: (0, 0))],
        out_specs=pl.BlockSpec((tm, tn), lambda i, j: (i, j)),
        out_shape=jax.ShapeDtypeStruct((T, (n_q_tiles + 1) * tn), BF16),
        compiler_params=_params("arbitrary", "arbitrary"),
        name="swa_qkv_proj",
    )(h, w, w, gain_row, gain_row, blk)


def _swa_kernel(q_ref, kp_ref, kc_ref, vp_ref, vc_ref, sink_ref, o_ref):
    step = pl.program_id(1)
    BLK, hd = SW_BLOCK, SW_HEAD_DIM
    G = SW_HEADS // SW_KV_HEADS
    pairs = G // 2
    band = 2 * BLK
    nq = q_ref.shape[0] // BLK
    qi = lax.broadcasted_iota(jnp.int32, (BLK, band), 0)
    kr = lax.broadcasted_iota(jnp.int32, (BLK, band), 1)
    rel = BLK + qi - kr
    in_window = (rel >= 0) & (rel < BLK)
    bias_any = jnp.where(in_window, 0.0, -jnp.inf)
    bias_first = jnp.where(in_window & (kr >= jnp.where(step > 0, 0, BLK)), 0.0, -jnp.inf)
    lane = lax.broadcasted_iota(jnp.int32, (1, LANES), 1)
    lo = (lane < hd).astype(BF16)
    hi = (lane >= hd).astype(BF16)
    lo_f = lane < hd
    ones_lo = jnp.broadcast_to(lo, (band, LANES))
    ones_hi = jnp.broadcast_to(hi, (band, LANES))

    def band_rows(prev_ref, cur_ref, qb, cols):
        before = prev_ref[:, cols] if qb == 0 else cur_ref[(qb - 1) * BLK:qb * BLK, cols]
        return jnp.concatenate([before, cur_ref[qb * BLK:(qb + 1) * BLK, cols]], axis=0)

    units = [(qb, kh, p) for qb in range(nq) for kh in range(SW_KV_HEADS) for p in range(pairs)]

    scores = {}
    for qb in range(nq):
        for kh in range(SW_KV_HEADS):
            kd = band_rows(kp_ref, kc_ref, qb, slice(kh * LANES, (kh + 1) * LANES))
            kz = jnp.concatenate([kd * lo, kd * hi], axis=0)
            for p in range(pairs):
                q = q_ref[qb * BLK:(qb + 1) * BLK, (kh * pairs + p) * LANES:(kh * pairs + p + 1) * LANES]
                scores[qb, kh, p] = lax.dot_general(q, kz, (((1,), (1,)), ((), ())), preferred_element_type=F32)

    vz = None
    for qb, kh, p in units:
        if p == 0:
            vd = band_rows(vp_ref, vc_ref, qb, slice(kh * LANES, (kh + 1) * LANES))
            vz = jnp.concatenate([jnp.concatenate([vd * lo, ones_lo], axis=1),
                                  jnp.concatenate([vd * hi, ones_hi], axis=1)], axis=0)
        bias = bias_first if qb == 0 else bias_any
        s = scores[qb, kh, p]
        sink_b = sink_ref[kh, p * BLK:(p + 1) * BLK, :]
        es, mxs = [], []
        for half in range(2):
            sh = s[:, half * band:(half + 1) * band] + bias
            mx = jnp.maximum(jnp.max(sh, axis=1, keepdims=True), sink_b[:, half * hd:half * hd + 1])
            es.append(jnp.exp(sh - mx).astype(BF16))
            mxs.append(mx)
        o2 = _dot(jnp.concatenate(es, axis=1), vz)
        o = o2[:, :LANES] / (o2[:, LANES:] + jnp.exp(sink_b - jnp.where(lo_f, mxs[0], mxs[1])))
        o_ref[qb * BLK:(qb + 1) * BLK, (kh * pairs + p) * LANES:(kh * pairs + p + 1) * LANES] = o.astype(BF16)


def _swa(qkv, sink_cols, batch, seq, q_blocks=4):
    T = qkv.shape[0]
    BLK = SW_BLOCK
    rows = q_blocks * BLK
    ns = seq // rows
    qw = SW_HEADS * SW_HEAD_DIM
    kw = SW_KV_HEADS * LANES
    k_off = qw // kw
    v_off = k_off + 1

    def cur(b, j):
        return b * ns + j

    def prev(b, j):
        return (b * ns + j) * q_blocks - jnp.minimum(j, 1)

    return pl.pallas_call(
        _swa_kernel,
        grid=(batch, ns),
        in_specs=[pl.BlockSpec((rows, qw), lambda b, j: (cur(b, j), 0)),
                  pl.BlockSpec((BLK, kw), lambda b, j: (prev(b, j), k_off)),
                  pl.BlockSpec((rows, kw), lambda b, j: (cur(b, j), k_off)),
                  pl.BlockSpec((BLK, kw), lambda b, j: (prev(b, j), v_off)),
                  pl.BlockSpec((rows, kw), lambda b, j: (cur(b, j), v_off)),
                  pl.BlockSpec(sink_cols.shape, lambda b, j: (0, 0, 0))],
        out_specs=pl.BlockSpec((rows, qw), lambda b, j: (cur(b, j), 0)),
        out_shape=jax.ShapeDtypeStruct((T, qw), BF16),
        compiler_params=_params("arbitrary", "arbitrary"),
        name="swa_attention",
    )(qkv, qkv, qkv, qkv, qkv, sink_cols)


def kernel(x, c, l0_w_mod, l0_b_mod, l0_w_in, l0_gate_bias, l0_h_norm, l0_w_out, l0_w_up, l0_conv_w, l0_conv_b,
           l0_w_down, l1_w_mod, l1_b_mod, l1_w_qkv, l1_q_norm, l1_k_norm, l1_sinks, l1_w_out, l1_w_up, l1_conv_w,
           l1_conv_b, l1_w_down):
    B, S, D = x.shape
    T = B * S
    H, dk, dv = ML_HEADS, ML_DQK, ML_DV
    x2d = x.reshape(T, D)

    mod0 = _modulation(c, l0_w_mod, l0_b_mod).reshape(B, 6, D)
    mod1 = _modulation(c, l1_w_mod, l1_b_mod).reshape(B, 6, D)

    n_main = 2 * H * dk + 2 * H * dv
    w_main = l0_w_in.astype(BF16)
    w_gate = jnp.pad(l0_w_in[:, n_main:], ((0, 0), (0, LANES - 2 * H))).astype(BF16)
    gate_bias = jnp.pad(l0_gate_bias, (0, LANES - 2 * H)).reshape(1, LANES)
    col_scale = jnp.concatenate([jnp.ones((H * dk,), F32), jnp.full((H * dk,), dk ** -0.5, F32),
                                 jnp.ones((2 * H * dv,), F32)]).reshape(1, n_main)
    hm = _first_norm(x2d, mod0, S)
    qkvo, gcol, grow = _inproj(hm, w_main, col_scale, w_gate, gate_bias)
    y = _mlstm(qkvo, gcol, grow, l0_h_norm.reshape(1, H * dv), B, S)
    x1, hf = _outproj(y, l0_w_out, x2d, mod0, S)
    act = _ffn_up(hf, l0_w_up, l0_conv_w, l0_conv_b, S)
    x2, hm1 = _ffn_down(act, l0_w_down.astype(BF16), x1, mod0, mod1, S)

    Hq, Hk, hd = SW_HEADS, SW_KV_HEADS, SW_HEAD_DIM
    scale = hd ** -0.5

    gain = jnp.concatenate([jnp.tile(l1_q_norm * scale, Hq), jnp.tile(l1_k_norm, Hk),
                            jnp.ones((Hk * hd,), F32)]).reshape(1, -1)
    qkv = _qkv_proj(hm1, l1_w_qkv.astype(BF16), gain)
    pairs = Hq // Hk // 2
    sink_cols = jnp.repeat(jnp.repeat(l1_sinks.reshape(Hk, pairs, 2), SW_BLOCK, axis=1), hd, axis=2)
    att = _swa(qkv, sink_cols, B, S)
    x3, hf1 = _outproj(att, l1_w_out, x2, mod1, S)
    act1 = _ffn_up(hf1, l1_w_up, l1_conv_w, l1_conv_b, S)
    (x4,) = _ffn_down(act1, l1_w_down.astype(BF16), x3, mod1, None, S)
    return x4.reshape(B, S, D)
```

```python
import functools
import math

import jax
import jax.numpy as jnp
from jax import lax
from jax.experimental import pallas as pl
from jax.experimental.pallas import tpu as pltpu

EPS = 1e-6
ML_HEADS = 8
ML_DQK = 128
ML_DV = 256
GATE_CAP = 15.0
SW_HEADS = 32
SW_KV_HEADS = 4
SW_HEAD_DIM = 64
SW_BLOCK = 128
CONV_WIDTH = 3
MLSTM_CHUNK_UNROLL = 8
FFN_DOT_ROWS = 512
FFN_EPILOGUE_ROWS = 32

LANES = 128
SUBLANES = 8
VMEM_LIMIT = 56 * 1024 * 1024

F32 = jnp.float32
BF16 = jnp.bfloat16


def _params(*sem):
    return pltpu.CompilerParams(dimension_semantics=sem, vmem_limit_bytes=VMEM_LIMIT)


def _dot(a, b):
    return jnp.dot(a, b, preferred_element_type=F32)


def _rms_mod(x, scale_row, shift_row):
    ms = jnp.mean(x * x, axis=-1, keepdims=True)
    return (x * lax.rsqrt(ms + EPS)) * (1.0 + scale_row) + shift_row


def _mod_kernel(c_ref, w_ref, b_ref, o_ref):
    c = c_ref[...]
    sc = (c * jax.nn.sigmoid(c)).astype(BF16)
    o_ref[...] = _dot(sc, w_ref[...].astype(BF16)) + b_ref[...]


def _modulation(c, w_mod, b_mod, tn=1024):
    B, D = c.shape
    N = w_mod.shape[1]
    return pl.pallas_call(
        _mod_kernel,
        grid=(N // tn,),
        in_specs=[pl.BlockSpec((B, D), lambda j: (0, 0)),
                  pl.BlockSpec((D, tn), lambda j: (0, j)),
                  pl.BlockSpec((1, tn), lambda j: (0, j))],
        out_specs=pl.BlockSpec((B, tn), lambda j: (0, j)),
        out_shape=jax.ShapeDtypeStruct((B, N), F32),
        compiler_params=_params("arbitrary"),
        name="modulation",
    )(c, w_mod, b_mod.reshape(1, N))


def _norm_kernel(x_ref, mod_ref, o_ref):
    o_ref[...] = _rms_mod(x_ref[...], mod_ref[0, 1:2, :], mod_ref[0, 0:1, :]).astype(BF16)


def _first_norm(x2d, mod, seq, tm=1024):
    T, D = x2d.shape
    per = seq // tm
    return pl.pallas_call(
        _norm_kernel,
        grid=(T // tm,),
        in_specs=[pl.BlockSpec((tm, D), lambda i: (i, 0)),
                  pl.BlockSpec((1, 6, D), lambda i: (i // per, 0, 0))],
        out_specs=pl.BlockSpec((tm, D), lambda i: (i, 0)),
        out_shape=jax.ShapeDtypeStruct((T, D), BF16),
        compiler_params=_params("arbitrary"),
        name="first_norm",
    )(x2d, mod)


def _inproj_kernel(h_ref, w_ref, cs_ref, wg_ref, gb_ref, o_ref, gcol_ref, grow_ref):
    j = pl.program_id(1)
    h = h_ref[...]
    o_ref[...] = (_dot(h, w_ref[...]) * cs_ref[...]).astype(BF16)

    @pl.when(j == 0)
    def _():
        g = _dot(h, wg_ref[...]) + gb_ref[...]
        g = GATE_CAP * jnp.tanh(g * (1.0 / GATE_CAP))
        log_f = jnp.minimum(g, 0.0) - jnp.log(1.0 + jnp.exp(-jnp.abs(g)))
        lane = lax.broadcasted_iota(jnp.int32, g.shape, 1)
        g = jnp.where(lane < ML_HEADS, g, log_f)
        gcol_ref[...] = g
        grow_ref[...] = jnp.transpose(g)[: 2 * ML_HEADS, :]


def _inproj(h, w_main, col_scale, w_gate, gate_bias, tm=2048, tn=1024):
    T, D = h.shape
    N = col_scale.shape[1]
    return pl.pallas_call(
        _inproj_kernel,
        grid=(T // tm, N // tn),
        in_specs=[pl.BlockSpec((tm, D), lambda i, j: (i, 0)),
                  pl.BlockSpec((D, tn), lambda i, j: (0, j)),
                  pl.BlockSpec((1, tn), lambda i, j: (0, j)),
                  pl.BlockSpec((D, LANES), lambda i, j: (0, 0)),
                  pl.BlockSpec((1, LANES), lambda i, j: (0, 0))],
        out_specs=[pl.BlockSpec((tm, tn), lambda i, j: (i, j)),
                   pl.BlockSpec((tm, LANES), lambda i, j: (i, 0)),
                   pl.BlockSpec((2 * ML_HEADS, tm), lambda i, j: (0, i))],
        out_shape=[jax.ShapeDtypeStruct((T, N), BF16),
                   jax.ShapeDtypeStruct((T, LANES), F32),
                   jax.ShapeDtypeStruct((2 * ML_HEADS, T), F32)],
        compiler_params=_params("arbitrary", "arbitrary"),
        name="mlstm_inproj",
    )(h, w_main, col_scale, w_gate, gate_bias)


def _split_bf16(x):
    hi = x.astype(BF16)
    lo = (x - hi.astype(F32)).astype(BF16)
    return hi, lo


def _mlstm_kernel(q_ref, k_ref, v_ref, o_ref, gcol_ref, grow_ref, hn_ref, y_ref,
                  dexp_ref, wint_ref, floor_ref, wcol_ref, decay_ref, cn_ref,
                  *, heads_per_step, seq):
    L = LANES
    hg = heads_per_step
    nc = seq // L
    dk, dv = ML_DQK, ML_DV
    h0 = pl.program_id(1) * hg
    row = lax.broadcasted_iota(jnp.int32, (L, L), 0)
    col = lax.broadcasted_iota(jnp.int32, (L, L), 1)
    causal = col <= row
    tril = jnp.where(causal, 1.0, 0.0).astype(BF16)
    triu = jnp.where(row <= col, 1.0, 0.0).astype(BF16)

    sel_r = lax.broadcasted_iota(jnp.int32, (L, hg * L), 0)
    sel_head = h0 + lax.broadcasted_iota(jnp.int32, (L, hg * L), 1) // L
    sel_i = jnp.where(sel_r == sel_head, 1.0, 0.0).astype(BF16)
    sel_f = jnp.where(sel_r == sel_head + ML_HEADS, 1.0, 0.0).astype(BF16)
    gate_row = lax.broadcasted_iota(jnp.int32, (2 * ML_HEADS, 1), 0)
    m_run = [jnp.zeros((1, L), F32) for _ in range(hg)]
    spread = []
    for c in range(nc):
        rows = slice(c * L, (c + 1) * L)
        terms = _split_bf16(gcol_ref[rows, :])
        ib_all = sum(_dot(t, sel_i) for t in terms)
        f_terms = [_dot(t, sel_f).astype(BF16) for t in terms]
        grow = grow_ref[:, rows]
        cgrow = sum(_dot(t, triu) for t in _split_bf16(grow))
        spread.append((ib_all, f_terms, grow, cgrow))
    for c in range(nc):
        ib_all, f_terms, grow, cgrow = spread[c]
        bb_all = sum(_dot(tril, t) for t in f_terms)
        for hh in range(hg):
            head = h0 + hh
            m_c = m_run[hh]
            ib, bb = ib_all[:, hh * L:(hh + 1) * L], bb_all[:, hh * L:(hh + 1) * L]
            a_row = jnp.sum(jnp.where(gate_row == head, grow, 0.0)
                            - jnp.where(gate_row == head + ML_HEADS, cgrow, 0.0),
                            axis=0, keepdims=True)
            ca = jnp.max(jnp.where(causal, a_row, -jnp.inf), axis=1, keepdims=True)
            mm = jnp.maximum(m_c, jnp.broadcast_to(ca, (L, L)))
            total = bb[L - 1:L, :]
            m_next = total + mm[L - 1:L, :]
            dexp_ref[hh, c] = jnp.exp(jnp.where(causal, a_row - mm, -jnp.inf))
            wint_ref[hh, c] = jnp.exp(m_c - mm)
            floor_ref[hh, c] = jnp.exp(-(bb + mm))
            wcol_ref[hh, c] = jnp.exp(total + (ib - bb) - m_next)
            decay_ref[hh, c] = jnp.broadcast_to(jnp.exp(total + m_c - m_next), (SUBLANES, L))
            m_run[hh] = m_next

    cn_ref[...] = jnp.zeros_like(cn_ref)
    ones = jnp.ones((L, L), BF16)

    def chunk_step(ci, carry):
        r0 = pl.multiple_of(ci * L, L)
        early = []
        for hh in range(hg):
            q = q_ref[pl.ds(r0, L), hh * dk:(hh + 1) * dk]
            k = k_ref[pl.ds(r0, L), hh * dk:(hh + 1) * dk]
            v = v_ref[pl.ds(r0, L), hh * dv:(hh + 1) * dv]
            v_ext = jnp.concatenate([v, ones], axis=1)
            cn = cn_ref[hh]
            k_w = (k.astype(F32) * wcol_ref[hh, ci]).astype(BF16)
            kv = lax.dot_general(k_w, v_ext, (((0,), (0,)), ((), ())), preferred_element_type=F32)
            s = lax.dot_general(q, k, (((1,), (1,)), ((), ())), preferred_element_type=F32)
            inter = _dot(q, cn.astype(BF16))
            dec = decay_ref[hh, ci][0:1, :]
            cn_ref[hh] = jnp.concatenate([dec] * (cn.shape[1] // L), axis=1) * cn + kv
            early.append((s, inter, v_ext))
        for hh in range(hg):
            s, inter, v_ext = early[hh]
            p = (dexp_ref[hh, ci] * s).astype(BF16)
            intra = _dot(p, v_ext)
            wint = wint_ref[hh, ci]
            num = jnp.concatenate([wint, wint], axis=1) * inter[:, :dv] + intra[:, :dv]
            den = wint * inter[:, dv:] + intra[:, dv:]
            rdn = 1.0 / jnp.maximum(jnp.abs(den), floor_ref[hh, ci])
            hc = num * jnp.concatenate([rdn, rdn], axis=1)
            half_gain = 0.5 * hn_ref[:, hh * dv:(hh + 1) * dv]
            hs = hc * lax.rsqrt(jnp.mean(hc * hc, axis=1, keepdims=True) + EPS) * half_gain
            og = o_ref[pl.ds(r0, L), hh * dv:(hh + 1) * dv].astype(F32)
            y_ref[pl.ds(r0, L), hh * dv:(hh + 1) * dv] = ((1.0 + jnp.tanh(0.5 * og)) * hs).astype(BF16)
        return carry

    lax.fori_loop(0, nc, chunk_step, 0, unroll=MLSTM_CHUNK_UNROLL)


def _mlstm(qkvo, gcol, grow, h_norm, batch, seq, heads_per_step=2):
    T = qkvo.shape[0]
    H, dk, dv = ML_HEADS, ML_DQK, ML_DV
    assert dk == LANES and seq % LANES == 0
    hg = heads_per_step
    nc = seq // LANES
    assert nc % MLSTM_CHUNK_UNROLL == 0
    qb, vb = hg * dk, hg * dv
    k_off = (H * dk) // qb
    v_off = (2 * H * dk) // vb
    o_off = (2 * H * dk + H * dv) // vb
    kern = functools.partial(_mlstm_kernel, heads_per_step=hg, seq=seq)
    factor = pltpu.VMEM((hg, nc, LANES, LANES), F32)
    return pl.pallas_call(
        kern,
        grid=(batch, H // hg),
        in_specs=[pl.BlockSpec((seq, qb), lambda b, g: (b, g)),
                  pl.BlockSpec((seq, qb), lambda b, g: (b, k_off + g)),
                  pl.BlockSpec((seq, vb), lambda b, g: (b, v_off + g)),
                  pl.BlockSpec((seq, vb), lambda b, g: (b, o_off + g)),
                  pl.BlockSpec((seq, LANES), lambda b, g: (b, 0)),
                  pl.BlockSpec((2 * H, seq), lambda b, g: (0, b)),
                  pl.BlockSpec((1, vb), lambda b, g: (0, g))],
        out_specs=pl.BlockSpec((seq, vb), lambda b, g: (b, g)),
        out_shape=jax.ShapeDtypeStruct((T, H * dv), BF16),
        scratch_shapes=[factor, factor, factor, factor,
                        pltpu.VMEM((hg, nc, SUBLANES, LANES), F32),
                        pltpu.VMEM((hg, dk, dv + LANES), F32)],
        compiler_params=_params("arbitrary", "arbitrary"),
        name="mlstm_chunks",
    )(qkvo, qkvo, qkvo, qkvo, gcol, grow, h_norm)


def _outproj_kernel(y_ref, w_ref, x_ref, mod_ref, xo_ref, ho_ref):
    acc = _dot(y_ref[...], w_ref[...].astype(BF16))
    x_new = x_ref[...] + mod_ref[0, 2:3, :] * acc
    xo_ref[...] = x_new
    ho_ref[...] = _rms_mod(x_new, mod_ref[0, 4:5, :], mod_ref[0, 3:4, :]).astype(BF16)


def _outproj(y, w, x2d, mod, seq, tm=512):
    T, K = y.shape
    D = w.shape[1]
    per = seq // tm
    return pl.pallas_call(
        _outproj_kernel,
        grid=(T // tm,),
        in_specs=[pl.BlockSpec((tm, K), lambda i: (i, 0)),
                  pl.BlockSpec((K, D), lambda i: (0, 0), pipeline_mode=pl.Buffered(1)),
                  pl.BlockSpec((tm, D), lambda i: (i, 0)),
                  pl.BlockSpec((1, 6, D), lambda i: (i // per, 0, 0))],
        out_specs=[pl.BlockSpec((tm, D), lambda i: (i, 0)),
                   pl.BlockSpec((tm, D), lambda i: (i, 0))],
        out_shape=[jax.ShapeDtypeStruct((T, D), F32),
                   jax.ShapeDtypeStruct((T, D), BF16)],
        compiler_params=_params("arbitrary"),
        name="outproj_residual_norm",
    )(y, w, x2d, mod)


def _ffn_up_kernel(h_ref, wg_ref, wv_ref, cwg_ref, cwv_ref, cbg_ref, cbv_ref, o_ref, ug_ref, uv_ref,
                   wbg_ref, wbv_ref, *, seq, sub):
    rb, rc = FFN_DOT_ROWS, FFN_EPILOGUE_ROWS
    units = [(s, b0) for s in range(o_ref.shape[1] // sub) for b0 in range(0, seq, rb)]

    def matmul(n, u_ref, w_ref, wb_ref):
        s, b0 = units[n]
        if b0 == 0:
            wb_ref[s % 2] = w_ref[:, s * sub:(s + 1) * sub].astype(BF16)
        u_ref[n % 2, :SUBLANES, :] = (u_ref[(n - 1) % 2, rb:, :] if b0 else jnp.zeros((SUBLANES, sub), F32))
        u_ref[n % 2, SUBLANES:, :] = _dot(h_ref[b0:b0 + rb, :], wb_ref[s % 2])

    def conv(u_ref, n, r0, cw, cb):
        out = cb + cw[2:3, :] * u_ref[n % 2, SUBLANES + r0:SUBLANES + r0 + rc, :]
        out = out + cw[1:2, :] * u_ref[n % 2, SUBLANES - 1 + r0:SUBLANES - 1 + r0 + rc, :]
        out = out + cw[0:1, :] * u_ref[n % 2, SUBLANES - 2 + r0:SUBLANES - 2 + r0 + rc, :]
        return out

    def epilogue(n, half):
        s, b0 = units[n]
        cols = slice(s * sub, (s + 1) * sub)
        cwg, cbg = 0.5 * cwg_ref[:, cols], 0.5 * cbg_ref[:, cols]
        cwv, cbv = cwv_ref[:, cols], cbv_ref[:, cols]
        for r0 in range(half * rb // 2, (half + 1) * rb // 2, rc):
            half_g = conv(ug_ref, n, r0, cwg, cbg)
            v = conv(uv_ref, n, r0, cwv, cbv)
            o_ref[b0 + r0:b0 + r0 + rc, cols] = ((half_g + half_g * jnp.tanh(half_g)) * v).astype(BF16)

    for n in range(len(units) + 1):
        for half, (u_ref, w_ref, wb_ref) in enumerate(((ug_ref, wg_ref, wbg_ref), (uv_ref, wv_ref, wbv_ref))):
            if n < len(units):
                matmul(n, u_ref, w_ref, wb_ref)
            if n:
                epilogue(n - 1, half)


def _ffn_up(h, w_up, conv_w, conv_b, seq, tn=512, sub=256):
    T, D = h.shape
    F = w_up.shape[1] // 2
    nf = F // tn
    kern = functools.partial(_ffn_up_kernel, seq=seq, sub=sub)
    cb = conv_b.reshape(1, 2 * F)
    return pl.pallas_call(
        kern,
        grid=(T // seq, nf),
        in_specs=[pl.BlockSpec((seq, D), lambda i, j: (i, 0)),
                  pl.BlockSpec((D, tn), lambda i, j: (0, j)),
                  pl.BlockSpec((D, tn), lambda i, j: (0, nf + j)),
                  pl.BlockSpec((CONV_WIDTH, tn), lambda i, j: (0, j)),
                  pl.BlockSpec((CONV_WIDTH, tn), lambda i, j: (0, nf + j)),
                  pl.BlockSpec((1, tn), lambda i, j: (0, j)),
                  pl.BlockSpec((1, tn), lambda i, j: (0, nf + j))],
        out_specs=pl.BlockSpec((seq, tn), lambda i, j: (i, j)),
        out_shape=jax.ShapeDtypeStruct((T, F), BF16),
        scratch_shapes=[pltpu.VMEM((2, SUBLANES + FFN_DOT_ROWS, sub), F32),
                        pltpu.VMEM((2, SUBLANES + FFN_DOT_ROWS, sub), F32),
                        pltpu.VMEM((2, D, sub), BF16), pltpu.VMEM((2, D, sub), BF16)],
        compiler_params=_params("arbitrary", "arbitrary"),
        name="ffn_up_conv_act",
    )(h, w_up, w_up, conv_w, conv_w, cb, cb)


def _ffn_down_kernel(a_ref, w_ref, x_ref, mod_ref, *rest, emit_norm):
    x_new = x_ref[...] + mod_ref[0, 5:6, :] * _dot(a_ref[...], w_ref[...])
    if emit_norm:
        nmod_ref, xo_ref, ho_ref = rest
        ho_ref[...] = _rms_mod(x_new, nmod_ref[0, 1:2, :], nmod_ref[0, 0:1, :]).astype(BF16)
    else:
        (xo_ref,) = rest
    xo_ref[...] = x_new


def _ffn_down(a, w, x2d, mod, next_mod, seq, tm=512):
    T, K = a.shape
    D = w.shape[1]
    per = seq // tm
    emit_norm = next_mod is not None
    mod_spec = pl.BlockSpec((1, 6, D), lambda i: (i // per, 0, 0))
    row_spec = pl.BlockSpec((tm, D), lambda i: (i, 0))
    in_specs = [pl.BlockSpec((tm, K), lambda i: (i, 0)),
                pl.BlockSpec((K, D), lambda i: (0, 0), pipeline_mode=pl.Buffered(1)),
                row_spec, mod_spec]
    args = [a, w, x2d, mod]
    out_specs = [row_spec]
    out_shape = [jax.ShapeDtypeStruct((T, D), F32)]
    if emit_norm:
        in_specs.append(mod_spec)
        args.append(next_mod)
        out_specs.append(row_spec)
        out_shape.append(jax.ShapeDtypeStruct((T, D), BF16))
    return pl.pallas_call(
        functools.partial(_ffn_down_kernel, emit_norm=emit_norm),
        grid=(T // tm,),
        in_specs=in_specs,
        out_specs=out_specs,
        out_shape=out_shape,
        compiler_params=_params("arbitrary"),
        name="ffn_down_residual_norm",
    )(*args)


def _qkv_kernel(h_ref, wa_ref, wb_ref, ga_ref, gb_ref, blk_ref, o_ref, *, n_q_tiles, sub):
    j = pl.program_id(1)
    h = h_ref[...]
    hd = SW_HEAD_DIM
    half = wa_ref.shape[1]

    def normed(acc, gain):
        ss = _dot((acc * acc).astype(BF16), blk_ref[...])
        return acc * lax.rsqrt(ss * (1.0 / hd) + EPS) * gain

    def duplicated(x):
        lo = lax.broadcasted_iota(jnp.int32, (1, LANES), 1) < hd
        out = []
        for c0 in range(0, x.shape[1], LANES):
            xc = x[:, c0:c0 + LANES]
            rolled = pltpu.roll(xc, hd, axis=1)
            out += [jnp.where(lo, xc, rolled), jnp.where(lo, rolled, xc)]
        return jnp.concatenate(out, axis=1)

    @pl.when(j < n_q_tiles)
    def _():
        srcs = [(w_ref, g_ref, slice(s, s + sub)) for w_ref, g_ref in ((wa_ref, ga_ref), (wb_ref, gb_ref))
                for s in range(0, half, sub)]
        accs = [_dot(h, w_ref[:, cols]) for w_ref, _, cols in srcs]
        for p, (_, g_ref, cols) in enumerate(srcs):
            o_ref[:, p * sub:(p + 1) * sub] = normed(accs[p], g_ref[:, cols]).astype(BF16)

    @pl.when(j == n_q_tiles)
    def _():
        k = _dot(h, wa_ref[:, :sub])
        v = _dot(h, wa_ref[:, sub:])
        o_ref[:, :half] = duplicated(normed(k, ga_ref[:, :sub])).astype(BF16)
        o_ref[:, half:] = duplicated(v).astype(BF16)


def _qkv_proj(h, w, gain_row, tm=2048, tn=1024, sub=256):
    T, D = h.shape
    Hq, Hk, hd = SW_HEADS, SW_KV_HEADS, SW_HEAD_DIM
    assert sub == Hk * hd and (Hq * hd) % tn == 0 and tn == 4 * sub
    n_q_tiles = Hq * hd // tn
    half = tn // 2
    kv_blk = Hq * hd // half
    r = lax.broadcasted_iota(jnp.int32, (sub, sub), 0) // hd
    c = lax.broadcasted_iota(jnp.int32, (sub, sub), 1) // hd
    blk = (r == c).astype(BF16)
    kern = functools.partial(_qkv_kernel, n_q_tiles=n_q_tiles, sub=sub)

    def first(i, j):
        return (0, jnp.minimum(2 * j, kv_blk))

    def second(i, j):
        return (0, jnp.minimum(2 * j + 1, kv_blk))

    return pl.pallas_call(
        kern,
        grid=(T // tm, n_q_tiles + 1),
        in_specs=[pl.BlockSpec((tm, D), lambda i, j: (i, 0)),
                  pl.BlockSpec((D, half), first),
                  pl.BlockSpec((D, half), second),
                  pl.BlockSpec((1, half), first),
                  pl.BlockSpec((1, half), second),
                  pl.BlockSpec((sub, sub), lambda i, j: (0, 0))],
        out_specs=pl.BlockSpec((tm, tn), lambda i, j: (i, j)),
        out_shape=jax.ShapeDtypeStruct((T, (n_q_tiles + 1) * tn), BF16),
        compiler_params=_params("arbitrary", "arbitrary"),
        name="swa_qkv_proj",
    )(h, w, w, gain_row, gain_row, blk)


def _swa_kernel(q_ref, kp_ref, kc_ref, vp_ref, vc_ref, sink_ref, o_ref):
    step = pl.program_id(1)
    BLK, hd = SW_BLOCK, SW_HEAD_DIM
    G = SW_HEADS // SW_KV_HEADS
    pairs = G // 2
    band = 2 * BLK
    nq = q_ref.shape[0] // BLK
    qi = lax.broadcasted_iota(jnp.int32, (BLK, band), 0)
    kr = lax.broadcasted_iota(jnp.int32, (BLK, band), 1)
    rel = BLK + qi - kr
    in_window = (rel >= 0) & (rel < BLK)
    bias_any = jnp.where(in_window, 0.0, -jnp.inf)
    bias_first = jnp.where(in_window & (kr >= jnp.where(step > 0, 0, BLK)), 0.0, -jnp.inf)
    lane = lax.broadcasted_iota(jnp.int32, (1, LANES), 1)
    lo = (lane < hd).astype(BF16)
    hi = (lane >= hd).astype(BF16)
    lo_f = lane < hd
    ones_lo = jnp.broadcast_to(lo, (band, LANES))
    ones_hi = jnp.broadcast_to(hi, (band, LANES))

    def band_rows(prev_ref, cur_ref, qb, cols):
        before = prev_ref[:, cols] if qb == 0 else cur_ref[(qb - 1) * BLK:qb * BLK, cols]
        return jnp.concatenate([before, cur_ref[qb * BLK:(qb + 1) * BLK, cols]], axis=0)

    units = [(qb, kh, p) for qb in range(nq) for kh in range(SW_KV_HEADS) for p in range(pairs)]

    scores = {}
    for qb in range(nq):
        for kh in range(SW_KV_HEADS):
            kd = band_rows(kp_ref, kc_ref, qb, slice(kh * LANES, (kh + 1) * LANES))
            kz = jnp.concatenate([kd * lo, kd * hi], axis=0)
            for p in range(pairs):
                q = q_ref[qb * BLK:(qb + 1) * BLK, (kh * pairs + p) * LANES:(kh * pairs + p + 1) * LANES]
                scores[qb, kh, p] = lax.dot_general(q, kz, (((1,), (1,)), ((), ())), preferred_element_type=F32)

    vz = None
    for qb, kh, p in units:
        if p == 0:
            vd = band_rows(vp_ref, vc_ref, qb, slice(kh * LANES, (kh + 1) * LANES))
            vz = jnp.concatenate([jnp.concatenate([vd * lo, ones_lo], axis=1),
                                  jnp.concatenate([vd * hi, ones_hi], axis=1)], axis=0)
        bias = bias_first if qb == 0 else bias_any
        s = scores[qb, kh, p]
        sink_b = sink_ref[kh, p * BLK:(p + 1) * BLK, :]
        es, mxs = [], []
        for half in range(2):
            sh = s[:, half * band:(half + 1) * band] + bias
            mx = jnp.maximum(jnp.max(sh, axis=1, keepdims=True), sink_b[:, half * hd:half * hd + 1])
            es.append(jnp.exp(sh - mx).astype(BF16))
            mxs.append(mx)
        o2 = _dot(jnp.concatenate(es, axis=1), vz)
        o = o2[:, :LANES] / (o2[:, LANES:] + jnp.exp(sink_b - jnp.where(lo_f, mxs[0], mxs[1])))
        o_ref[qb * BLK:(qb + 1) * BLK, (kh * pairs + p) * LANES:(kh * pairs + p + 1) * LANES] = o.astype(BF16)


def _swa(qkv, sink_cols, batch, seq, q_blocks=4):
    T = qkv.shape[0]
    BLK = SW_BLOCK
    rows = q_blocks * BLK
    ns = seq // rows
    qw = SW_HEADS * SW_HEAD_DIM
    kw = SW_KV_HEADS * LANES
    k_off = qw // kw
    v_off = k_off + 1

    def cur(b, j):
        return b * ns + j

    def prev(b, j):
        return (b * ns + j) * q_blocks - jnp.minimum(j, 1)

    return pl.pallas_call(
        _swa_kernel,
        grid=(batch, ns),
        in_specs=[pl.BlockSpec((rows, qw), lambda b, j: (cur(b, j), 0)),
                  pl.BlockSpec((BLK, kw), lambda b, j: (prev(b, j), k_off)),
                  pl.BlockSpec((rows, kw), lambda b, j: (cur(b, j), k_off)),
                  pl.BlockSpec((BLK, kw), lambda b, j: (prev(b, j), v_off)),
                  pl.BlockSpec((rows, kw), lambda b, j: (cur(b, j), v_off)),
                  pl.BlockSpec(sink_cols.shape, lambda b, j: (0, 0, 0))],
        out_specs=pl.BlockSpec((rows, qw), lambda b, j: (cur(b, j), 0)),
        out_shape=jax.ShapeDtypeStruct((T, qw), BF16),
        compiler_params=_params("arbitrary", "arbitrary"),
        name="swa_attention",
    )(qkv, qkv, qkv, qkv, qkv, sink_cols)


def kernel(x, c, l0_w_mod, l0_b_mod, l0_w_in, l0_gate_bias, l0_h_norm, l0_w_out, l0_w_up, l0_conv_w, l0_conv_b,
           l0_w_down, l1_w_mod, l1_b_mod, l1_w_qkv, l1_q_norm, l1_k_norm, l1_sinks, l1_w_out, l1_w_up, l1_conv_w,
           l1_conv_b, l1_w_down):
    B, S, D = x.shape
    T = B * S
    H, dk, dv = ML_HEADS, ML_DQK, ML_DV
    x2d = x.reshape(T, D)

    mod0 = _modulation(c, l0_w_mod, l0_b_mod).reshape(B, 6, D)
    mod1 = _modulation(c, l1_w_mod, l1_b_mod).reshape(B, 6, D)

    n_main = 2 * H * dk + 2 * H * dv
    w_main = l0_w_in.astype(BF16)
    w_gate = jnp.pad(l0_w_in[:, n_main:], ((0, 0), (0, LANES - 2 * H))).astype(BF16)
    gate_bias = jnp.pad(l0_gate_bias, (0, LANES - 2 * H)).reshape(1, LANES)
    col_scale = jnp.concatenate([jnp.ones((H * dk,), F32), jnp.full((H * dk,), dk ** -0.5, F32),
                                 jnp.ones((2 * H * dv,), F32)]).reshape(1, n_main)
    hm = _first_norm(x2d, mod0, S)
    qkvo, gcol, grow = _inproj(hm, w_main, col_scale, w_gate, gate_bias)
    y = _mlstm(qkvo, gcol, grow, l0_h_norm.reshape(1, H * dv), B, S)
    x1, hf = _outproj(y, l0_w_out, x2d, mod0, S)
    act = _ffn_up(hf, l0_w_up, l0_conv_w, l0_conv_b, S)
    x2, hm1 = _ffn_down(act, l0_w_down.astype(BF16), x1, mod0, mod1, S)

    Hq, Hk, hd = SW_HEADS, SW_KV_HEADS, SW_HEAD_DIM
    scale = hd ** -0.5

    gain = jnp.concatenate([jnp.tile(l1_q_norm * scale, Hq), jnp.tile(l1_k_norm, Hk),
                            jnp.ones((Hk * hd,), F32)]).reshape(1, -1)
    qkv = _qkv_proj(hm1, l1_w_qkv.astype(BF16), gain)
    pairs = Hq // Hk // 2
    sink_cols = jnp.repeat(jnp.repeat(l1_sinks.reshape(Hk, pairs, 2), SW_BLOCK, axis=1), hd, axis=2)
    att = _swa(qkv, sink_cols, B, S)
    x3, hf1 = _outproj(att, l1_w_out, x2, mod1, S)
    act1 = _ffn_up(hf1, l1_w_up, l1_conv_w, l1_conv_b, S)
    (x4,) = _ffn_down(act1, l1_w_down.astype(BF16), x3, mod1, None, S)
    return x4.reshape(B, S, D)
```
